```python
import jax, jax.numpy as jnp
from jax import lax
import numpy as np

D_MODEL = 1024
BATCH = 2
SEQ = 16384
DEPTH = 2

GRID_W = 64
CTX_LEN = 256
D_RNN = D_MODEL
RNN_HEADS = 16
RNN_HEAD_DIM = D_RNN // RNN_HEADS
RNN_CONV_W = 4
RNN_PAD_L = 2
RNN_PAD_R = 1
LRU_C = 8.0
D_CONV = D_MODEL
CONV_W = 31
CONV_PAD = CONV_W // 2
D_FF = 2816
N_EXPERTS = 8
TOP_K = 2
N_BRANCH = 2
N_DENSE = (DEPTH + 1) // 2
N_MOE = DEPTH // 2
EPS = 1e-6
IN_COLS = 2 * D_RNN + 2 * D_CONV + N_BRANCH * D_MODEL
SPLITS = (D_RNN, 2 * D_RNN, 2 * D_RNN + D_CONV, 2 * D_RNN + 2 * D_CONV)

kernel_name = 'hybrid_rglru_conformer_moe_dit'


def _rmsnorm(x, g):
    xf = x.astype(jnp.float32)
    y = xf * lax.rsqrt(jnp.mean(xf * xf, axis=-1, keepdims=True) + EPS)
    return (y * g.astype(jnp.float32)).astype(x.dtype)


def _layernorm(x, g, b):
    xf = x.astype(jnp.float32)
    mu = jnp.mean(xf, axis=-1, keepdims=True)
    var = jnp.mean(jnp.square(xf - mu), axis=-1, keepdims=True)
    y = (xf - mu) * lax.rsqrt(var + EPS) * g.astype(jnp.float32) + b.astype(jnp.float32)
    return y.astype(x.dtype)


def _modulate(h, shift, scale):
    return h * (1 + scale) + shift


def _dwconv_seq(u, w, b, pad_l, pad_r):
    y = lax.conv_general_dilated(u, w[:, None, :].astype(u.dtype), window_strides=(1,),
                                 padding=((pad_l, pad_r),), dimension_numbers=('NWC', 'WIO', 'NWC'),
                                 feature_group_count=u.shape[-1])
    return y + b


def _dwconv_rows(u, w, b):
    bsz, L, C = u.shape
    rows = L // GRID_W
    u4 = u.reshape(bsz, rows, GRID_W, C)
    y = lax.conv_general_dilated(u4, w[:, None, None, :].astype(u.dtype), window_strides=(1, 1),
                                 padding=((CONV_PAD, CONV_PAD), (0, 0)),
                                 dimension_numbers=('NHWC', 'HWIO', 'NHWC'), feature_group_count=C)
    return y.reshape(bsz, L, C) + b


def _lin_scan(a, b, reverse, h0=None):
    def combine(first, second):
        a1, b1 = first
        a2, b2 = second
        return a1 * a2, a2 * b1 + b2
    a_cum, h = lax.associative_scan(combine, (a, b), reverse=reverse, axis=1)
    if h0 is None:
        return h
    return h + a_cum * h0[:, None, :]


def _rglru_dir(uc, wr, br, wi, bi, lam, reverse, h0=None):
    bsz, L, _ = uc.shape
    uh = uc.reshape(bsz, L, RNN_HEADS, RNN_HEAD_DIM)
    r = jax.nn.sigmoid(jnp.einsum('blhi,hij->blhj', uh, wr).reshape(bsz, L, D_RNN) + br)
    gi = jax.nn.sigmoid(jnp.einsum('blhi,hij->blhj', uh, wi).reshape(bsz, L, D_RNN) + bi)
    log_a = -LRU_C * r.astype(jnp.float32) * jax.nn.softplus(-lam.astype(jnp.float32))
    a = jnp.exp(log_a)
    b = jnp.sqrt(-jnp.expm1(2.0 * log_a)) * (gi * uc).astype(jnp.float32)
    return _lin_scan(a, b, reverse, h0)


def _rglru_bidir(u, conv_w, conv_b, wr, br, wi, bi, lam, h0_f=None, h0_b=None):
    uc = _dwconv_seq(u, conv_w, conv_b, RNN_PAD_L, RNN_PAD_R)
    hf = _rglru_dir(uc, wr[0], br[0], wi[0], bi[0], lam[0], False, h0_f)
    hb = _rglru_dir(uc, wr[1], br[1], wi[1], bi[1], lam[1], True, h0_b)
    return hf, hb


def _mixer(h, w_in, rnn_conv_w, rnn_conv_b, wr, br, wi, bi, lam, conv_w, conv_b, ln_g, ln_b,
           w_a, w_b, w_out, on_grid, h0_f=None, h0_b=None):
    z = h @ w_in
    u_rnn, g_rnn, u_glu, g_glu, g_mrg = jnp.split(z, SPLITS, axis=-1)
    hf, hb = _rglru_bidir(u_rnn, rnn_conv_w, rnn_conv_b, wr, br, wi, bi, lam, h0_f, h0_b)
    y_a = ((hf + hb).astype(h.dtype) * jax.nn.gelu(g_rnn)) @ w_a
    v = u_glu * jax.nn.sigmoid(g_glu)
    v = _dwconv_rows(v, conv_w, conv_b) if on_grid else _dwconv_seq(v, conv_w, conv_b, CONV_PAD, CONV_PAD)
    y_b = jax.nn.silu(_layernorm(v, ln_g, ln_b)) @ w_b
    g_a, g_b = jnp.split(jax.nn.sigmoid(g_mrg), N_BRANCH, axis=-1)
    out = (g_a * y_a + g_b * y_b) @ w_out
    return out, hf[:, -1], hb[:, 0]


def _swiglu(h, w1, w3, w2):
    return (jax.nn.silu(h @ w1) * (h @ w3)) @ w2


def _moe(h, router, w1, w3, w2):
    logits = (h @ router).astype(jnp.float32)
    top_v, top_i = lax.top_k(logits, TOP_K)
    weights = jax.nn.softmax(top_v, axis=-1)
    combine = jnp.sum(jax.nn.one_hot(top_i, N_EXPERTS, dtype=jnp.float32) * weights[..., None], axis=-2)
    combine = combine.astype(h.dtype)
    out = jnp.zeros_like(h)
    for e in range(N_EXPERTS):
        out = out + combine[..., e:e + 1] * _swiglu(h, w1[e], w3[e], w2[e])
    return out


def _ffn(h, i, ffn_w1, ffn_w3, ffn_w2, moe_router, moe_w1, moe_w3, moe_w2):
    j = i // 2
    if i % 2 == 0:
        return _swiglu(h, ffn_w1[j], ffn_w3[j], ffn_w2[j])
    return _moe(h, moe_router[j], moe_w1[j], moe_w3[j], moe_w2[j])


def setup_inputs(seed: int = 0) -> dict:
    key = jax.random.key(seed)
    ks = iter(jax.random.split(key, 48))
    D = D_MODEL

    def nrm(shape, scale):
        return scale * jax.random.normal(next(ks), shape, jnp.float32)

    a_init = jax.random.uniform(next(ks), (DEPTH, 2, D_RNN), jnp.float32, 0.9, 0.999)
    return {
        'x': nrm((BATCH, SEQ, D), 1.0),
        'c': nrm((BATCH, D), 1.0),
        'ctx': nrm((BATCH, CTX_LEN, D), 1.0),
        'c_ctx': nrm((D,), 1.0),
        'mod_w': nrm((DEPTH, D, 6 * D), 0.5 * D ** -0.5),
        'mod_b': nrm((DEPTH, 6 * D), 0.02),
        'norm1_g': 1.0 + nrm((DEPTH, D), 0.1),
        'norm2_g': 1.0 + nrm((DEPTH, D), 0.1),
        'w_in': nrm((DEPTH, D, IN_COLS), D ** -0.5),
        'rnn_conv_w': nrm((DEPTH, RNN_CONV_W, D_RNN), RNN_CONV_W ** -0.5),
        'rnn_conv_b': nrm((DEPTH, D_RNN), 0.02),
        'lru_wr': nrm((DEPTH, 2, RNN_HEADS, RNN_HEAD_DIM, RNN_HEAD_DIM), RNN_HEAD_DIM ** -0.5),
        'lru_br': nrm((DEPTH, 2, D_RNN), 0.02),
        'lru_wi': nrm((DEPTH, 2, RNN_HEADS, RNN_HEAD_DIM, RNN_HEAD_DIM), RNN_HEAD_DIM ** -0.5),
        'lru_bi': nrm((DEPTH, 2, D_RNN), 0.02),
        'lru_lam': jnp.log(a_init) - jnp.log1p(-a_init),
        'conv_w': nrm((DEPTH, CONV_W, D_CONV), CONV_W ** -0.5),
        'conv_b': nrm((DEPTH, D_CONV), 0.02),
        'conv_ln_g': 1.0 + nrm((DEPTH, D_CONV), 0.1),
        'conv_ln_b': nrm((DEPTH, D_CONV), 0.02),
        'w_branch_a': nrm((DEPTH, D_RNN, D), D_RNN ** -0.5),
        'w_branch_b': nrm((DEPTH, D_CONV, D), D_CONV ** -0.5),
        'w_out': nrm((DEPTH, D, D), D ** -0.5),
        'ffn_w1': nrm((N_DENSE, D, D_FF), D ** -0.5),
        'ffn_w3': nrm((N_DENSE, D, D_FF), D ** -0.5),
        'ffn_w2': nrm((N_DENSE, D_FF, D), D_FF ** -0.5),
        'moe_router': nrm((N_MOE, D, N_EXPERTS), D ** -0.5),
        'moe_w1': nrm((N_MOE, N_EXPERTS, D, D_FF), D ** -0.5),
        'moe_w3': nrm((N_MOE, N_EXPERTS, D, D_FF), D ** -0.5),
        'moe_w2': nrm((N_MOE, N_EXPERTS, D_FF, D), D_FF ** -0.5),
        'final_g': 1.0 + nrm((D,), 0.1),
    }


def reference(x, c, ctx, c_ctx, mod_w, mod_b, norm1_g, norm2_g, w_in, rnn_conv_w, rnn_conv_b,
              lru_wr, lru_br, lru_wi, lru_bi, lru_lam, conv_w, conv_b, conv_ln_g, conv_ln_b,
              w_branch_a, w_branch_b, w_out, ffn_w1, ffn_w3, ffn_w2, moe_router, moe_w1, moe_w3,
              moe_w2, final_g):
    sc = jax.nn.silu(c)
    scc = jax.nn.silu(c_ctx)
    for i in range(DEPTH):
        last = i == DEPTH - 1
        mod_x = jnp.split((sc @ mod_w[i] + mod_b[i])[:, None, :], 6, axis=-1)
        mod_c = jnp.split((scc @ mod_w[i] + mod_b[i])[None, None, :], 6, axis=-1)
        lru = (lru_wr[i], lru_br[i], lru_wi[i], lru_bi[i], lru_lam[i])
        mix_p = (w_in[i], rnn_conv_w[i], rnn_conv_b[i], *lru, conv_w[i], conv_b[i], conv_ln_g[i],
                 conv_ln_b[i], w_branch_a[i], w_branch_b[i], w_out[i])
        hc = _modulate(_rmsnorm(ctx, norm1_g[i]), mod_c[0], mod_c[1])
        if last:
            hf_c, hb_c = _rglru_bidir(hc @ w_in[i][:, :D_RNN], rnn_conv_w[i], rnn_conv_b[i], *lru)
            s_f, s_b = hf_c[:, -1], hb_c[:, 0]
        else:
            out_c, s_f, s_b = _mixer(hc, *mix_p, False)
            ctx = ctx + mod_c[2] * out_c
            hc2 = _modulate(_rmsnorm(ctx, norm2_g[i]), mod_c[3], mod_c[4])
            ctx = ctx + mod_c[5] * _ffn(hc2, i, ffn_w1, ffn_w3, ffn_w2, moe_router, moe_w1, moe_w3, moe_w2)
        hx = _modulate(_rmsnorm(x, norm1_g[i]), mod_x[0], mod_x[1])
        out_x, _, _ = _mixer(hx, *mix_p, True, s_f, s_b)
        x = x + mod_x[2] * out_x
        hx2 = _modulate(_rmsnorm(x, norm2_g[i]), mod_x[3], mod_x[4])
        x = x + mod_x[5] * _ffn(hx2, i, ffn_w1, ffn_w3, ffn_w2, moe_router, moe_w1, moe_w3, moe_w2)
    return _rmsnorm(x, final_g)
```

```python
import functools

import jax
import jax.numpy as jnp
from jax import lax
from jax.experimental import pallas as pl
from jax.experimental.pallas import tpu as pltpu

EPS = 1e-6
LRU_C = 8.0
GRID_W = 64
RNN_PAD_L = 2
GATE_GROUP = 256
SUBLANES = 8
LANES = 128
BF16_ROWS = 16
VMEM_LIMIT = 56 * 1024 * 1024

F32 = jnp.float32
BF16 = jnp.bfloat16


def _sigmoid(x):
    return 0.5 * jnp.tanh(0.5 * x) + 0.5


def _gelu_tanh(x):
    return 0.5 * x * (1.0 + jnp.tanh(0.7978845608028654 * (x + 0.044715 * (x * x * x))))


def _norm_mod(x, g, shift, scale):
    ms = jnp.mean(x * x, axis=-1, keepdims=True)
    y = x * lax.rsqrt(ms + EPS) * g
    return y * (1.0 + scale) + shift


def _params(n_axes):
    return pltpu.CompilerParams(dimension_semantics=("arbitrary",) * n_axes, vmem_limit_bytes=VMEM_LIMIT)


def _const_spec(shape):
    nd = len(shape)
    return pl.BlockSpec(shape, lambda *_: (0,) * nd, pipeline_mode=pl.Buffered(1))


def _mod_kernel(rows_ref, w_ref, b_ref, o_ref):
    r = rows_ref[...]
    s = r * _sigmoid(r)
    o_ref[0] = jnp.dot(s, w_ref[0], precision=lax.Precision.HIGHEST, preferred_element_type=F32) + b_ref[0]


def _mod_call(rows, mod_w, mod_b):
    depth, d, n6 = mod_w.shape
    tn = d
    return pl.pallas_call(
        _mod_kernel,
        grid=(depth, n6 // tn),
        in_specs=[
            pl.BlockSpec((SUBLANES, d), lambda l, n: (0, 0)),
            pl.BlockSpec((1, d, tn), lambda l, n: (l, 0, n)),
            pl.BlockSpec((1, 1, tn), lambda l, n: (l, 0, n)),
        ],
        out_specs=pl.BlockSpec((1, SUBLANES, tn), lambda l, n: (l, 0, n)),
        out_shape=jax.ShapeDtypeStruct((depth, SUBLANES, n6), F32),
        compiler_params=_params(2),
        name="adaln_mod",
    )(rows, mod_w, mod_b.reshape(depth, 1, n6))


def _inproj_kernel(x_ref, xp_ref, xn_ref, mod_ref, g_ref, w_ref, cw_ref, cb_ref,
                   uc_ref, gg_ref, v_ref, ga_ref, gb_ref, zbuf, *, T, D, KW):
    t = pl.program_id(1)
    nt = pl.num_programs(1)
    shift = mod_ref[0, 0:1, :]
    scale = mod_ref[0, 1:2, :]
    g = g_ref[...]
    h = _norm_mod(x_ref[0], g, shift, scale).astype(BF16)
    hp = _norm_mod(xp_ref[0], g, shift, scale).astype(BF16)
    hn = _norm_mod(xn_ref[0], g, shift, scale).astype(BF16)

    w0 = w_ref[:, 0:D]
    zbuf[0:SUBLANES, :] = jnp.dot(hp, w0, preferred_element_type=F32) * (t > 0).astype(F32)
    zbuf[SUBLANES:SUBLANES + T, :] = jnp.dot(h, w0, preferred_element_type=F32)
    zbuf[SUBLANES + T:2 * SUBLANES + T, :] = jnp.dot(hn, w0, preferred_element_type=F32) * (t < nt - 1).astype(F32)
    uc = jnp.broadcast_to(cb_ref[...], (T, D))
    for k in range(KW):
        uc = uc + cw_ref[k:k + 1, :] * zbuf[SUBLANES - RNN_PAD_L + k:SUBLANES - RNN_PAD_L + k + T, :]
    uc_ref[0] = uc.astype(BF16)

    gg_ref[0] = _gelu_tanh(jnp.dot(h, w_ref[:, D:2 * D], preferred_element_type=F32)).astype(BF16)
    vg = jnp.dot(h, w_ref[:, 2 * D:3 * D], preferred_element_type=F32)
    gate = _sigmoid(jnp.dot(h, w_ref[:, 3 * D:4 * D], preferred_element_type=F32))
    v_ref[0] = (vg * gate).astype(BF16)
    ga_ref[0] = _sigmoid(jnp.dot(h, w_ref[:, 4 * D:5 * D], preferred_element_type=F32)).astype(BF16)
    gb_ref[0] = _sigmoid(jnp.dot(h, w_ref[:, 5 * D:6 * D], preferred_element_type=F32)).astype(BF16)


def _inproj_call(x, mod, g1, w_in, cw, cb, T):
    B, L, D = x.shape
    KW = cw.shape[0]
    nt = L // T
    per_batch = mod.shape[0] > 1
    tb = T // SUBLANES
    last = L // SUBLANES - 1
    tile = pl.BlockSpec((1, T, D), lambda b, t: (b, t, 0))
    out_sds = jax.ShapeDtypeStruct((B, L, D), BF16)
    return pl.pallas_call(
        functools.partial(_inproj_kernel, T=T, D=D, KW=KW),
        grid=(B, nt),
        in_specs=[
            tile,
            pl.BlockSpec((1, SUBLANES, D), lambda b, t: (b, jnp.maximum(t * tb - 1, 0), 0)),
            pl.BlockSpec((1, SUBLANES, D), lambda b, t: (b, jnp.minimum((t + 1) * tb, last), 0)),
            pl.BlockSpec((1, 6, D), (lambda b, t: (b, 0, 0)) if per_batch else (lambda b, t: (0, 0, 0))),
            _const_spec((1, D)),
            _const_spec(w_in.shape),
            _const_spec((KW, D)),
            _const_spec((1, D)),
        ],
        out_specs=[tile] * 5,
        out_shape=[out_sds] * 5,
        scratch_shapes=[pltpu.VMEM((T + 2 * SUBLANES, D), F32)],
        compiler_params=_params(2),
        name="in_projection",
    )(x, x, x, mod, g1, w_in, cw, cb)


def _convb_kernel(v_ref, vp_ref, vn_ref, cw_ref, cb_ref, lg_ref, lb_ref, wb_ref, gmb_ref, o_ref,
                  vbuf, ybuf, *, T, D, HB, S, RC, KW):
    t = pl.program_id(1)
    nt = pl.num_programs(1)
    vbuf[0:HB, :] = vp_ref[0].astype(F32) * (t > 0).astype(F32)
    vbuf[HB:HB + T, :] = v_ref[0].astype(F32)
    vbuf[HB + T:2 * HB + T, :] = vn_ref[0].astype(F32) * (t < nt - 1).astype(F32)
    pad = KW // 2

    def chunk(r0):
        acc = jnp.broadcast_to(cb_ref[...], (RC, D))
        for k in range(KW):
            acc = acc + cw_ref[k:k + 1, :] * vbuf[pl.ds(r0 + (HB + (k - pad) * S), RC), :]
        mu = jnp.mean(acc, axis=-1, keepdims=True)
        xc = acc - mu
        var = jnp.mean(xc * xc, axis=-1, keepdims=True)
        y = xc * lax.rsqrt(var + EPS) * lg_ref[...] + lb_ref[...]
        ybuf[pl.ds(r0, RC), :] = (y * _sigmoid(y)).astype(BF16)

    n_chunks = T // RC
    if S % SUBLANES == 0:
        def body(i, carry):
            chunk(pl.multiple_of(i * RC, RC))
            return carry
        lax.fori_loop(0, n_chunks, body, 0)
    else:
        for i in range(n_chunks):
            chunk(i * RC)
    yb = jnp.dot(ybuf[...], wb_ref[...], preferred_element_type=F32)
    o_ref[0] = (yb * gmb_ref[0].astype(F32)).astype(BF16)


def _convb_call(v, gmb, cw, cb, lg, lb, wb, T, HB, S):
    B, L, D = v.shape
    KW = cw.shape[0]
    assert (KW // 2) * S <= HB and T % HB == 0 and L % T == 0
    nt = L // T
    r = T // HB
    last = L // HB - 1
    tile = pl.BlockSpec((1, T, D), lambda b, t: (b, t, 0))
    return pl.pallas_call(
        functools.partial(_convb_kernel, T=T, D=D, HB=HB, S=S, RC=BF16_ROWS, KW=KW),
        grid=(B, nt),
        in_specs=[
            tile,
            pl.BlockSpec((1, HB, D), lambda b, t: (b, jnp.maximum(t * r - 1, 0), 0)),
            pl.BlockSpec((1, HB, D), lambda b, t: (b, jnp.minimum((t + 1) * r, last), 0)),
            _const_spec((KW, D)),
            _const_spec((1, D)),
            _const_spec((1, D)),
            _const_spec((1, D)),
            _const_spec((D, D)),
            tile,
        ],
        out_specs=tile,
        out_shape=jax.ShapeDtypeStruct((B, L, D), BF16),
        scratch_shapes=[pltpu.VMEM((T + 2 * HB, D), F32), pltpu.VMEM((T, D), BF16)],
        compiler_params=_params(2),
        name="conv_branch",
    )(v, v, v, cw, cb, lg, lb, wb, gmb)


def _scan_rows(a, b, h, reverse):
    sub = lax.broadcasted_iota(jnp.int32, a.shape, 0)
    for k in (1, 2, 4):
        if reverse:
            ar, br, m = pltpu.roll(a, SUBLANES - k, 0), pltpu.roll(b, SUBLANES - k, 0), sub < SUBLANES - k
        else:
            ar, br, m = pltpu.roll(a, k, 0), pltpu.roll(b, k, 0), sub >= k
        b = jnp.where(m, a * br + b, b)
        a = jnp.where(m, a * ar, a)
    hh = b + a * h
    edge = hh[0:1, :] if reverse else hh[SUBLANES - 1:SUBLANES, :]
    return hh, jnp.broadcast_to(edge, hh.shape)


def _rnn_kernel(*refs, T, D, reverse, final, write_h):
    uc_ref, wg_ref, br_ref, bi_ref, lam_ref, h0_ref = refs[:6]
    pos = 6
    if final:
        hf_ref, gg_ref, ga_ref, yb_ref, x_ref, mod_ref, wa_ref, wo_ref = refs[pos:pos + 8]
        pos += 8
    outs = []
    if write_h:
        outs.append(refs[pos]); pos += 1
    st_ref = refs[pos]; pos += 1
    if final:
        xo_ref = refs[pos]; pos += 1
    a_buf, b_buf, hcar = refs[pos:pos + 3]

    t = pl.program_id(1)

    @pl.when(t == 0)
    def _():
        hcar[...] = jnp.broadcast_to(h0_ref[0], (SUBLANES, D))

    z = -lam_ref[...]
    clam = -LRU_C * (jnp.maximum(z, 0.0) + jnp.log1p(jnp.exp(-jnp.abs(z))))
    GW = GATE_GROUP
    for g in range(D // GW):
        sl = slice(g * GW, (g + 1) * GW)
        ug = uc_ref[0, :, sl]
        pre = jnp.dot(ug, wg_ref[g], preferred_element_type=F32)
        r = _sigmoid(pre[:, :GW] + br_ref[:, sl])
        gi = _sigmoid(pre[:, GW:] + bi_ref[:, sl])
        a = jnp.exp(r * clam[:, sl])
        a_buf[:, sl] = a
        b_buf[:, sl] = jnp.sqrt(1.0 - a * a) * (gi * ug.astype(F32))

    n_blk = T // SUBLANES

    def body(i, h):
        blk = (n_blk - 1 - i) if reverse else i
        rows = pl.ds(pl.multiple_of(blk * SUBLANES, SUBLANES), SUBLANES)
        hh, h_next = _scan_rows(a_buf[rows, :], b_buf[rows, :], h, reverse)
        b_buf[rows, :] = hh
        return h_next

    h_end = lax.fori_loop(0, n_blk, body, hcar[...])
    hcar[...] = h_end
    st_ref[0] = h_end[0:1, :]
    if write_h:
        outs[0][0] = b_buf[...].astype(BF16)
    if final:
        y = ((hf_ref[0].astype(F32) + b_buf[...]) * gg_ref[0].astype(F32)).astype(BF16)
        ya = jnp.dot(y, wa_ref[...], preferred_element_type=F32)
        m = (ga_ref[0].astype(F32) * ya + yb_ref[0].astype(F32)).astype(BF16)
        out = jnp.dot(m, wo_ref[...], preferred_element_type=F32)
        xo_ref[0] = x_ref[0] + mod_ref[0, 2:3, :] * out


def _rnn_call(uc, wg, br, bi, lam, h0, T, *, reverse, final_args=None, write_h=True):
    B, L, D = uc.shape
    nt = L // T
    final = final_args is not None
    tmap = (lambda b, t: (b, nt - 1 - t, 0)) if reverse else (lambda b, t: (b, t, 0))
    tile = pl.BlockSpec((1, T, D), tmap)
    per_batch_h0 = h0.shape[0] > 1
    in_specs = [
        tile,
        _const_spec(wg.shape),
        _const_spec((1, D)),
        _const_spec((1, D)),
        _const_spec((1, D)),
        pl.BlockSpec((1, 1, D), (lambda b, t: (b, 0, 0)) if per_batch_h0 else (lambda b, t: (0, 0, 0))),
    ]
    args = [uc, wg, br, bi, lam, h0]
    out_specs, out_shape = [], []
    if write_h:
        out_specs.append(tile)
        out_shape.append(jax.ShapeDtypeStruct((B, L, D), BF16))
    out_specs.append(pl.BlockSpec((1, 1, D), lambda b, t: (b, 0, 0)))
    out_shape.append(jax.ShapeDtypeStruct((B, 1, D), F32))
    if final:
        hf, gg, ga, yb, x, mod, wa, wo = final_args
        per_batch = mod.shape[0] > 1
        in_specs += [tile, tile, tile, tile, tile,
                     pl.BlockSpec((1, 6, D), (lambda b, t: (b, 0, 0)) if per_batch else (lambda b, t: (0, 0, 0))),
                     _const_spec((D, D)), _const_spec((D, D))]
        args += [hf, gg, ga, yb, x, mod, wa, wo]
        out_specs.append(tile)
        out_shape.append(jax.ShapeDtypeStruct((B, L, D), F32))
    return pl.pallas_call(
        functools.partial(_rnn_kernel, T=T, D=D, reverse=reverse, final=final, write_h=write_h),
        grid=(B, nt),
        in_specs=in_specs,
        out_specs=out_specs,
        out_shape=out_shape,
        scratch_shapes=[pltpu.VMEM((T, D), F32), pltpu.VMEM((T, D), F32), pltpu.VMEM((SUBLANES, D), F32)],
        compiler_params=_params(2),
        name="rnn_bwd_out" if final else ("rnn_bwd" if reverse else "rnn_fwd"),
    )(*args)


def _swiglu_rows(hb, w1_ref, w3_ref, w2_ref, FC):
    F = w1_ref.shape[-1]
    acc = None
    for c in range(F // FC):
        sl = slice(c * FC, (c + 1) * FC)
        a = jnp.dot(hb, w1_ref[:, sl], preferred_element_type=F32)
        b = jnp.dot(hb, w3_ref[:, sl], preferred_element_type=F32)
        act = (a * _sigmoid(a) * b).astype(BF16)
        part = jnp.dot(act, w2_ref[sl, :], preferred_element_type=F32)
        acc = part if acc is None else acc + part
    return acc


def _ffn_kernel(x_ref, mod_ref, g_ref, w1_ref, w3_ref, w2_ref, *rest, FC, final_norm):
    if final_norm:
        fg_ref, o_ref = rest
    else:
        (o_ref,) = rest
    x = x_ref[0]
    hb = _norm_mod(x, g_ref[...], mod_ref[0, 3:4, :], mod_ref[0, 4:5, :]).astype(BF16)
    y = x + mod_ref[0, 5:6, :] * _swiglu_rows(hb, w1_ref, w3_ref, w2_ref, FC)
    if final_norm:
        y = y * lax.rsqrt(jnp.mean(y * y, axis=-1, keepdims=True) + EPS) * fg_ref[...]
    o_ref[0] = y


def _ffn_call(x, mod, g2, w1, w3, w2, T, final_g=None):
    B, L, D = x.shape
    F = w1.shape[-1]
    per_batch = mod.shape[0] > 1
    tile = pl.BlockSpec((1, T, D), lambda b, t: (b, t, 0))
    in_specs = [tile,
                pl.BlockSpec((1, 6, D), (lambda b, t: (b, 0, 0)) if per_batch else (lambda b, t: (0, 0, 0))),
                _const_spec((1, D)), _const_spec((D, F)), _const_spec((D, F)), _const_spec((F, D))]
    args = [x, mod, g2, w1, w3, w2]
    if final_g is not None:
        in_specs.append(_const_spec((1, D)))
        args.append(final_g)
    return pl.pallas_call(
        functools.partial(_ffn_kernel, FC=_ff_chunk(F), final_norm=final_g is not None),
        grid=(B, L // T),
        in_specs=in_specs,
        out_specs=tile,
        out_shape=jax.ShapeDtypeStruct((B, L, D), F32),
        compiler_params=_params(2),
        name="dense_swiglu",
    )(*args)


def _ff_chunk(F):
    for fc in (512, 256, 128):
        if F % fc == 0:
            return fc
    return F


def _route_kernel(x_ref, mod_ref, g_ref, rw_ref, hb_ref, pos_ref, post_ref, comb_ref, cnt_ref, *, T, E):
    h = _norm_mod(x_ref[...], g_ref[...], mod_ref[0, 3:4, :], mod_ref[0, 4:5, :])
    hb_ref[...] = h.astype(BF16)
    logits = jnp.dot(h, rw_ref[...], precision=lax.Precision.HIGHEST, preferred_element_type=F32)
    lane = lax.broadcasted_iota(jnp.int32, logits.shape, 1).astype(F32)
    neg = jnp.float32(-jnp.inf)
    lg = jnp.where(lane < E, logits, neg)
    m1 = jnp.max(lg, axis=-1, keepdims=True)
    i1 = jnp.min(jnp.where(lg == m1, lane, float(LANES)), axis=-1, keepdims=True)
    lg2 = jnp.where(lane == i1, neg, lg)
    m2 = jnp.max(lg2, axis=-1, keepdims=True)
    i2 = jnp.min(jnp.where(lg2 == m2, lane, float(LANES)), axis=-1, keepdims=True)
    e2 = jnp.exp(m2 - m1)
    den = 1.0 + e2
    sel1 = lane == i1
    sel2 = lane == i2
    comb_ref[...] = jnp.where(sel1, 1.0 / den, jnp.where(sel2, e2 / den, 0.0))
    sel = jnp.where(sel1, 1.0, jnp.where(sel2, 1.0, 0.0))
    row = lax.broadcasted_iota(jnp.int32, (T, T), 0)
    col = lax.broadcasted_iota(jnp.int32, (T, T), 1)
    tri = jnp.where(col < row, 1.0, 0.0).astype(BF16)
    rank = jnp.dot(tri, sel.astype(BF16), preferred_element_type=F32)
    pos = jnp.where(sel > 0.0, rank, -1.0)
    pos_ref[...] = pos
    post_ref[0] = jnp.transpose(pos)[0:SUBLANES, :]
    cnt_ref[0] = jnp.sum(sel, axis=0, keepdims=True)


def _route_call(x2, mod, g2, rw, T, tiles_per_batch):
    N, D = x2.shape
    E = rw.shape[1]
    assert E <= SUBLANES
    nT = N // T
    rw_pad = jnp.zeros((D, LANES), F32).at[:, :E].set(rw)
    per_batch = mod.shape[0] > 1
    tile = pl.BlockSpec((T, D), lambda j: (j, 0))
    col = pl.BlockSpec((T, LANES), lambda j: (j, 0))
    return pl.pallas_call(
        functools.partial(_route_kernel, T=T, E=E),
        grid=(nT,),
        in_specs=[tile,
                  pl.BlockSpec((1, 6, D), (lambda j: (j // tiles_per_batch, 0, 0)) if per_batch else (lambda j: (0, 0, 0))),
                  _const_spec((1, D)), _const_spec((D, LANES))],
        out_specs=[tile, col, pl.BlockSpec((1, SUBLANES, T), lambda j: (j, 0, 0)), col,
                   pl.BlockSpec((1, 1, LANES), lambda j: (j, 0, 0))],
        out_shape=[jax.ShapeDtypeStruct((N, D), BF16), jax.ShapeDtypeStruct((N, LANES), F32),
                   jax.ShapeDtypeStruct((nT, SUBLANES, T), F32), jax.ShapeDtypeStruct((N, LANES), F32),
                   jax.ShapeDtypeStruct((nT, 1, LANES), F32)],
        compiler_params=_params(1),
        name="moe_route",
    )(x2, mod, g2, rw_pad)


def _compact_kernel(cnt_ref, start_ref, hb_ref, post_ref, hs_in_ref, hs_ref, buf, sem, *, T, E, CH):
    del hs_in_ref
    j = pl.program_id(0)
    hbt = hb_ref[...]
    riota = lax.broadcasted_iota(jnp.int32, (CH, T), 0).astype(F32)
    for e in range(E):
        n = cnt_ref[j * E + e]
        s0 = start_ref[j * E + e]
        pe = post_ref[0, e:e + 1, :]

        def chunk(c, carry, pe=pe, s0=s0):
            base = c * CH
            onehot = jnp.where(pe == riota + base.astype(F32), 1.0, 0.0).astype(BF16)
            buf[...] = jnp.dot(onehot, hbt, preferred_element_type=F32).astype(BF16)
            dst = hs_ref.at[pl.ds(pl.multiple_of(s0 + base, BF16_ROWS), CH), :]
            cp = pltpu.make_async_copy(buf, dst, sem)
            cp.start()
            cp.wait()
            return carry

        lax.fori_loop(0, (n + CH - 1) // CH, chunk, 0)


def _compact_call(cnt, start, hb, post, hs0, T, E, CH):
    N, D = hb.shape
    nT = N // T
    grid_spec = pltpu.PrefetchScalarGridSpec(
        num_scalar_prefetch=2,
        grid=(nT,),
        in_specs=[pl.BlockSpec((T, D), lambda j, *_: (j, 0)),
                  pl.BlockSpec((1, SUBLANES, T), lambda j, *_: (j, 0, 0)),
                  pl.BlockSpec(memory_space=pl.ANY)],
        out_specs=pl.BlockSpec(memory_space=pl.ANY),
        scratch_shapes=[pltpu.VMEM((CH, D), BF16), pltpu.SemaphoreType.DMA(())],
    )
    return pl.pallas_call(
        functools.partial(_compact_kernel, T=T, E=E, CH=CH),
        grid_spec=grid_spec,
        out_shape=jax.ShapeDtypeStruct(hs0.shape, BF16),
        input_output_aliases={4: 0},
        compiler_params=_params(1),
        name="moe_compact",
    )(cnt, start, hb, post, hs0)


def _gffn_kernel(te_ref, valid_ref, hs_ref, w1_ref, w3_ref, w2_ref, ys_ref, *, FC):
    del te_ref
    i = pl.program_id(0)

    @pl.when(valid_ref[i] != 0)
    def _():
        ys_ref[...] = _swiglu_rows(hs_ref[...], w1_ref.at[0], w3_ref.at[0], w2_ref.at[0], FC).astype(BF16)

    @pl.when(valid_ref[i] == 0)
    def _():
        ys_ref[...] = jnp.zeros(ys_ref.shape, BF16)


def _gffn_call(tile_expert, valid, hs, w1, w3, w2, TM):
    R, D = hs.shape
    E, _, F = w1.shape
    grid_spec = pltpu.PrefetchScalarGridSpec(
        num_scalar_prefetch=2,
        grid=(R // TM,),
        in_specs=[pl.BlockSpec((TM, D), lambda i, te, va: (i, 0)),
                  pl.BlockSpec((1, D, F), lambda i, te, va: (te[i], 0, 0)),
                  pl.BlockSpec((1, D, F), lambda i, te, va: (te[i], 0, 0)),
                  pl.BlockSpec((1, F, D), lambda i, te, va: (te[i], 0, 0))],
        out_specs=pl.BlockSpec((TM, D), lambda i, te, va: (i, 0)),
    )
    return pl.pallas_call(
        functools.partial(_gffn_kernel, FC=_ff_chunk(F)),
        grid_spec=grid_spec,
        out_shape=jax.ShapeDtypeStruct((R, D), BF16),
        compiler_params=_params(1),
        name="moe_grouped_swiglu",
    )(tile_expert, valid, hs, w1, w3, w2)


def _combine_kernel(cnt_ref, start_ref, x_ref, mod_ref, pos_ref, comb_ref, ys_ref, *rest, T, E, CH, final_norm):
    if final_norm:
        fg_ref, o_ref, buf, acc, sem = rest
    else:
        o_ref, buf, acc, sem = rest
    j = pl.program_id(0)
    acc[...] = jnp.zeros(acc.shape, F32)
    liota = lax.broadcasted_iota(jnp.int32, (T, CH), 1).astype(F32)
    for e in range(E):
        n = cnt_ref[j * E + e]
        s0 = start_ref[j * E + e]
        pe = pos_ref[:, e:e + 1]
        we = comb_ref[:, e:e + 1]

        def chunk(c, carry, pe=pe, we=we, s0=s0):
            base = c * CH
            src = ys_ref.at[pl.ds(pl.multiple_of(s0 + base, BF16_ROWS), CH), :]
            cp = pltpu.make_async_copy(src, buf, sem)
            cp.start()
            cp.wait()
            onehot = jnp.where(pe == liota + base.astype(F32), 1.0, 0.0).astype(BF16)
            acc[...] += we * jnp.dot(onehot, buf[...], preferred_element_type=F32)
            return carry

        lax.fori_loop(0, (n + CH - 1) // CH, chunk, 0)
    y = x_ref[...] + mod_ref[0, 5:6, :] * acc[...]
    if final_norm:
        y = y * lax.rsqrt(jnp.mean(y * y, axis=-1, keepdims=True) + EPS) * fg_ref[...]
    o_ref[...] = y


def _combine_call(cnt, start, x2, mod, pos, comb, ys, T, E, CH, tiles_per_batch, final_g=None):
    N, D = x2.shape
    per_batch = mod.shape[0] > 1
    tile = pl.BlockSpec((T, D), lambda j, *_: (j, 0))
    col = pl.BlockSpec((T, LANES), lambda j, *_: (j, 0))
    in_specs = [tile,
                pl.BlockSpec((1, 6, D), (lambda j, *_: (j // tiles_per_batch, 0, 0)) if per_batch
                             else (lambda j, *_: (0, 0, 0))),
                col, col, pl.BlockSpec(memory_space=pl.ANY)]
    args = [cnt, start, x2, mod, pos, comb, ys]
    if final_g is not None:
        in_specs.append(pl.BlockSpec((1, D), lambda j, *_: (0, 0)))
        args.append(final_g)
    grid_spec = pltpu.PrefetchScalarGridSpec(
        num_scalar_prefetch=2,
        grid=(N // T,),
        in_specs=in_specs,
        out_specs=tile,
        scratch_shapes=[pltpu.VMEM((CH, D), BF16), pltpu.VMEM((T, D), F32), pltpu.SemaphoreType.DMA(())],
    )
    return pl.pallas_call(
        functools.partial(_combine_kernel, T=T, E=E, CH=CH, final_norm=final_g is not None),
        grid_spec=grid_spec,
        out_shape=jax.ShapeDtypeStruct((N, D), F32),
        compiler_params=_params(1),
        name="moe_combine",
    )(*args)


def _moe(x, mod, g2, rw, w1, w3, w2, final_g=None):
    B, L, D = x.shape
    N = B * L
    E = rw.shape[1]
    T = min(512, L)
    CH = min(128, T)
    TM = 512
    nT = N // T
    x2 = x.reshape(N, D)
    hb, pos, post, comb, cnt_f = _route_call(x2, mod, g2, rw, T, L // T)

    cnt = cnt_f[:, 0, :E].astype(jnp.int32)
    npad = (cnt + BF16_ROWS - 1) // BF16_ROWS * BF16_ROWS
    glen = (npad.sum(0) + CH + TM - 1) // TM * TM
    gend = jnp.cumsum(glen)
    start = (gend - glen)[None, :] + jnp.cumsum(npad, axis=0) - npad
    rows_bound = 2 * N + nT * E * BF16_ROWS + E * (CH + TM)
    R = (rows_bound + TM - 1) // TM * TM
    tile_row = jnp.arange(R // TM, dtype=jnp.int32) * TM
    tile_expert = jnp.minimum(jnp.sum(tile_row[:, None] >= gend[None, :], axis=1), E - 1).astype(jnp.int32)
    valid = (tile_row < gend[-1]).astype(jnp.int32)
    cnt1, start1 = cnt.reshape(-1), start.reshape(-1).astype(jnp.int32)

    hs = _compact_call(cnt1, start1, hb, post, jnp.zeros((R, D), BF16), T, E, CH)
    ys = _gffn_call(tile_expert, valid, hs, w1, w3, w2, TM)
    out = _combine_call(cnt1, start1, x2, mod, pos, comb, ys, T, E, CH, L // T, final_g)
    return out.reshape(B, L, D)


def _pack_gates(wr, wi):
    H, hd, _ = wr.shape
    per = GATE_GROUP // hd
    G = H // per

    def blockdiag(w):
        w = w.reshape(G, per, hd, hd)
        eye = jnp.eye(per, dtype=w.dtype)
        return jnp.einsum('gpij,pq->gpiqj', w, eye).reshape(G, GATE_GROUP, GATE_GROUP)

    return jnp.concatenate([blockdiag(wr), blockdiag(wi)], axis=-1).astype(BF16)


def _mixer(x, mod, p, stream_tiles, conv_stride, h0_f, h0_b, full):
    T_in, T_rnn, T_cv, HB = stream_tiles
    uc, gg, v, ga, gb = _inproj_call(x, mod, p['g1'], p['w_in'], p['rnn_cw'], p['rnn_cb'], T_in)
    rnn_f = (uc, p['wg'][0], p['br'][0], p['bi'][0], p['lam'][0], h0_f, T_rnn)
    rnn_b = (uc, p['wg'][1], p['br'][1], p['bi'][1], p['lam'][1], h0_b, T_rnn)
    if not full:
        (s_f,) = _rnn_call(*rnn_f, reverse=False, write_h=False)
        (s_b,) = _rnn_call(*rnn_b, reverse=True, write_h=False)
        return None, s_f, s_b
    ybg = _convb_call(v, gb, p['cw'], p['cb'], p['lg'], p['lb'], p['w_b'], T_cv, HB, conv_stride)
    hf, s_f = _rnn_call(*rnn_f, reverse=False)
    s_b, x_new = _rnn_call(*rnn_b, reverse=True, write_h=False,
                           final_args=(hf, gg, ga, ybg, x, mod, p['w_a'], p['w_out']))
    return x_new, s_f, s_b


def kernel(x, c, ctx, c_ctx, mod_w, mod_b, norm1_g, norm2_g, w_in, rnn_conv_w, rnn_conv_b, lru_wr, lru_br, lru_wi, lru_bi, lru_lam, conv_w, conv_b, conv_ln_g, conv_ln_b, w_branch_a, w_branch_b, w_out, ffn_w1, ffn_w3, ffn_w2, moe_router, moe_w1, moe_w3, moe_w2, final_g):
    B, L, D = x.shape
    Lc = ctx.shape[1]
    depth = mod_w.shape[0]
    assert B + 1 <= SUBLANES and D % GATE_GROUP == 0 and GATE_GROUP % lru_wr.shape[-1] == 0

    rows = jnp.zeros((SUBLANES, D), F32).at[:B].set(c).at[B].set(c_ctx)
    mod_all = _mod_call(rows, mod_w, mod_b)

    lat_tiles = (min(512, L), min(512, L), min(1024, L), min(1024, L))
    ctx_tiles = (Lc, Lc, Lc, BF16_ROWS)
    T_ffn = min(512, L)
    zero_state = jnp.zeros((1, 1, D), F32)
    fg = final_g.reshape(1, D)

    for i in range(depth):
        last = i == depth - 1
        mod_x = mod_all[i, :B].reshape(B, 6, D)
        mod_c = mod_all[i, B].reshape(1, 6, D)
        p = dict(
            g1=norm1_g[i].reshape(1, D), w_in=w_in[i].astype(BF16),
            rnn_cw=rnn_conv_w[i], rnn_cb=rnn_conv_b[i].reshape(1, D),
            wg=[_pack_gates(lru_wr[i, d], lru_wi[i, d]) for d in range(2)],
            br=[lru_br[i, d].reshape(1, D) for d in range(2)], bi=[lru_bi[i, d].reshape(1, D) for d in range(2)],
            lam=[lru_lam[i, d].reshape(1, D) for d in range(2)],
            cw=conv_w[i], cb=conv_b[i].reshape(1, D), lg=conv_ln_g[i].reshape(1, D), lb=conv_ln_b[i].reshape(1, D),
            w_a=w_branch_a[i].astype(BF16), w_b=w_branch_b[i].astype(BF16), w_out=w_out[i].astype(BF16))
        g2 = norm2_g[i].reshape(1, D)
        j = i // 2
        dense = i % 2 == 0
        if dense:
            ffn_w = (ffn_w1[j].astype(BF16), ffn_w3[j].astype(BF16), ffn_w2[j].astype(BF16))
        else:
            ffn_w = (moe_router[j], moe_w1[j].astype(BF16), moe_w3[j].astype(BF16), moe_w2[j].astype(BF16))

        ctx_new, s_f, s_b = _mixer(ctx, mod_c, p, ctx_tiles, 1, zero_state, zero_state, full=not last)
        if not last:
            ctx = (_ffn_call(ctx_new, mod_c, g2, *ffn_w, Lc) if dense
                   else _moe(ctx_new, mod_c, g2, *ffn_w))
        x, _, _ = _mixer(x, mod_x, p, lat_tiles, GRID_W, s_f, s_b, full=True)
        fgi = fg if last else None
        x = (_ffn_call(x, mod_x, g2, *ffn_w, T_ffn, fgi) if dense else _moe(x, mod_x, g2, *ffn_w, fgi))
    return x
```

```python
import functools

import jax
import jax.numpy as jnp
from jax import lax
from jax.experimental import pallas as pl
from jax.experimental.pallas import tpu as pltpu

EPS = 1e-6
LRU_C = 8.0
GRID_W = 64
RNN_PAD_L = 2
GATE_GROUP = 256
SUBLANES = 8
LANES = 128
BF16_ROWS = 16
VMEM_LIMIT = 56 * 1024 * 1024

F32 = jnp.float32
BF16 = jnp.bfloat16


def _sigmoid(x):
    return 0.5 * jnp.tanh(0.5 * x) + 0.5


def _gelu_tanh(x):
    return 0.5 * x * (1.0 + jnp.tanh(0.7978845608028654 * (x + 0.044715 * (x * x * x))))


def _norm_mod(x, g, shift, scale):
    ms = jnp.mean(x * x, axis=-1, keepdims=True)
    y = x * lax.rsqrt(ms + EPS) * g
    return y * (1.0 + scale) + shift


def _params(n_axes):
    return pltpu.CompilerParams(dimension_semantics=("arbitrary",) * n_axes, vmem_limit_bytes=VMEM_LIMIT)


def _const_spec(shape):
    nd = len(shape)
    return pl.BlockSpec(shape, lambda *_: (0,) * nd, pipeline_mode=pl.Buffered(1))


def _mod_kernel(rows_ref, w_ref, b_ref, o_ref):
    r = rows_ref[...]
    s = r * _sigmoid(r)
    o_ref[0] = jnp.dot(s, w_ref[0], precision=lax.Precision.HIGHEST, preferred_element_type=F32) + b_ref[0]


def _mod_call(rows, mod_w, mod_b):
    depth, d, n6 = mod_w.shape
    tn = d
    return pl.pallas_call(
        _mod_kernel,
        grid=(depth, n6 // tn),
        in_specs=[
            pl.BlockSpec((SUBLANES, d), lambda l, n: (0, 0)),
            pl.BlockSpec((1, d, tn), lambda l, n: (l, 0, n)),
            pl.BlockSpec((1, 1, tn), lambda l, n: (l, 0, n)),
        ],
        out_specs=pl.BlockSpec((1, SUBLANES, tn), lambda l, n: (l, 0, n)),
        out_shape=jax.ShapeDtypeStruct((depth, SUBLANES, n6), F32),
        compiler_params=_params(2),
        name="adaln_mod",
    )(rows, mod_w, mod_b.reshape(depth, 1, n6))


def _inproj_kernel(x_ref, xp_ref, xn_ref, mod_ref, g_ref, w_ref, cw_ref, cb_ref,
                   uc_ref, gg_ref, v_ref, ga_ref, gb_ref, zbuf, *, T, D, KW):
    t = pl.program_id(1)
    nt = pl.num_programs(1)
    shift = mod_ref[0, 0:1, :]
    scale = mod_ref[0, 1:2, :]
    g = g_ref[...]
    h = _norm_mod(x_ref[0], g, shift, scale).astype(BF16)
    hp = _norm_mod(xp_ref[0], g, shift, scale).astype(BF16)
    hn = _norm_mod(xn_ref[0], g, shift, scale).astype(BF16)

    w0 = w_ref[:, 0:D]
    zbuf[0:SUBLANES, :] = jnp.dot(hp, w0, preferred_element_type=F32) * (t > 0).astype(F32)
    zbuf[SUBLANES:SUBLANES + T, :] = jnp.dot(h, w0, preferred_element_type=F32)
    zbuf[SUBLANES + T:2 * SUBLANES + T, :] = jnp.dot(hn, w0, preferred_element_type=F32) * (t < nt - 1).astype(F32)
    uc = jnp.broadcast_to(cb_ref[...], (T, D))
    for k in range(KW):
        uc = uc + cw_ref[k:k + 1, :] * zbuf[SUBLANES - RNN_PAD_L + k:SUBLANES - RNN_PAD_L + k + T, :]
    uc_ref[0] = uc.astype(BF16)

    gg_ref[0] = _gelu_tanh(jnp.dot(h, w_ref[:, D:2 * D], preferred_element_type=F32)).astype(BF16)
    vg = jnp.dot(h, w_ref[:, 2 * D:3 * D], preferred_element_type=F32)
    gate = _sigmoid(jnp.dot(h, w_ref[:, 3 * D:4 * D], preferred_element_type=F32))
    v_ref[0] = (vg * gate).astype(BF16)
    ga_ref[0] = _sigmoid(jnp.dot(h, w_ref[:, 4 * D:5 * D], preferred_element_type=F32)).astype(BF16)
    gb_ref[0] = _sigmoid(jnp.dot(h, w_ref[:, 5 * D:6 * D], preferred_element_type=F32)).astype(BF16)


def _inproj_call(x, mod, g1, w_in, cw, cb, T):
    B, L, D = x.shape
    KW = cw.shape[0]
    nt = L // T
    per_batch = mod.shape[0] > 1
    tb = T // SUBLANES
    last = L // SUBLANES - 1
    tile = pl.BlockSpec((1, T, D), lambda b, t: (b, t, 0))
    out_sds = jax.ShapeDtypeStruct((B, L, D), BF16)
    return pl.pallas_call(
        functools.partial(_inproj_kernel, T=T, D=D, KW=KW),
        grid=(B, nt),
        in_specs=[
            tile,
            pl.BlockSpec((1, SUBLANES, D), lambda b, t: (b, jnp.maximum(t * tb - 1, 0), 0)),
            pl.BlockSpec((1, SUBLANES, D), lambda b, t: (b, jnp.minimum((t + 1) * tb, last), 0)),
            pl.BlockSpec((1, 6, D), (lambda b, t: (b, 0, 0)) if per_batch else (lambda b, t: (0, 0, 0))),
            _const_spec((1, D)),
            _const_spec(w_in.shape),
            _const_spec((KW, D)),
            _const_spec((1, D)),
        ],
        out_specs=[tile] * 5,
        out_shape=[out_sds] * 5,
        scratch_shapes=[pltpu.VMEM((T + 2 * SUBLANES, D), F32)],
        compiler_params=_params(2),
        name="in_projection",
    )(x, x, x, mod, g1, w_in, cw, cb)


def _convb_kernel(v_ref, vp_ref, vn_ref, cw_ref, cb_ref, lg_ref, lb_ref, wb_ref, gmb_ref, o_ref,
                  vbuf, cbuf, ybuf, *, T, D, HB, S, RC, KW):
    t = pl.program_id(1)
    nt = pl.num_programs(1)
    vbuf[0:HB, :] = vp_ref[0].astype(F32) * (t > 0).astype(F32)
    vbuf[HB:HB + T, :] = v_ref[0].astype(F32)
    vbuf[HB + T:2 * HB + T, :] = vn_ref[0].astype(F32) * (t < nt - 1).astype(F32)
    pad = KW // 2

    def norm_act(r0, rows):
        x = cbuf[pl.ds(r0, rows), :]
        mu = jnp.mean(x, axis=-1, keepdims=True)
        xc = x - mu
        var = jnp.mean(xc * xc, axis=-1, keepdims=True)
        y = xc * lax.rsqrt(var + EPS) * lg_ref[...] + lb_ref[...]
        ybuf[pl.ds(r0, rows), :] = (y * _sigmoid(y)).astype(BF16)

    if S % SUBLANES == 0:
        CR = 2 * S

        def chunk(i, carry):
            r0 = pl.multiple_of(i * CR, CR)
            for lc in range(D // LANES):
                ln = slice(lc * LANES, (lc + 1) * LANES)
                acc0 = jnp.broadcast_to(cb_ref[:, ln], (S, LANES))
                acc1 = acc0
                for p in range(KW + 1):
                    piece = vbuf[pl.ds(r0 + (HB + (p - pad) * S), S), ln]
                    if p < KW:
                        acc0 = acc0 + cw_ref[p:p + 1, ln] * piece
                    if p >= 1:
                        acc1 = acc1 + cw_ref[p - 1:p, ln] * piece
                cbuf[pl.ds(r0, S), ln] = acc0
                cbuf[pl.ds(r0 + S, S), ln] = acc1
            norm_act(r0, CR)
            return carry

        lax.fori_loop(0, T // CR, chunk, 0)
    else:
        for i in range(T // RC):
            acc = jnp.broadcast_to(cb_ref[...], (RC, D))
            for k in range(KW):
                acc = acc + cw_ref[k:k + 1, :] * vbuf[i * RC + HB + (k - pad) * S:i * RC + HB + (k - pad) * S + RC, :]
            cbuf[i * RC:(i + 1) * RC, :] = acc
            norm_act(i * RC, RC)
    yb = jnp.dot(ybuf[...], wb_ref[...], preferred_element_type=F32)
    o_ref[0] = (yb * gmb_ref[0].astype(F32)).astype(BF16)


def _convb_call(v, gmb, cw, cb, lg, lb, wb, T, HB, S):
    B, L, D = v.shape
    KW = cw.shape[0]
    assert (KW // 2) * S <= HB and T % HB == 0 and L % T == 0 and T % (2 * S) == 0
    nt = L // T
    r = T // HB
    last = L // HB - 1
    tile = pl.BlockSpec((1, T, D), lambda b, t: (b, t, 0))
    return pl.pallas_call(
        functools.partial(_convb_kernel, T=T, D=D, HB=HB, S=S, RC=BF16_ROWS, KW=KW),
        grid=(B, nt),
        in_specs=[
            tile,
            pl.BlockSpec((1, HB, D), lambda b, t: (b, jnp.maximum(t * r - 1, 0), 0)),
            pl.BlockSpec((1, HB, D), lambda b, t: (b, jnp.minimum((t + 1) * r, last), 0)),
            _const_spec((KW, D)),
            _const_spec((1, D)),
            _const_spec((1, D)),
            _const_spec((1, D)),
            _const_spec((D, D)),
            tile,
        ],
        out_specs=tile,
        out_shape=jax.ShapeDtypeStruct((B, L, D), BF16),
        scratch_shapes=[pltpu.VMEM((T + 2 * HB, D), F32), pltpu.VMEM((T, D), F32), pltpu.VMEM((T, D), BF16)],
        compiler_params=_params(2),
        name="conv_branch",
    )(v, v, v, cw, cb, lg, lb, wb, gmb)


def _scan_rows(a, b, h, reverse):
    sub = lax.broadcasted_iota(jnp.int32, a.shape, 0)
    for k in (1, 2, 4):
        if reverse:
            ar, br, m = pltpu.roll(a, SUBLANES - k, 0), pltpu.roll(b, SUBLANES - k, 0), sub < SUBLANES - k
        else:
            ar, br, m = pltpu.roll(a, k, 0), pltpu.roll(b, k, 0), sub >= k
        b = jnp.where(m, a * br + b, b)
        a = jnp.where(m, a * ar, a)
    hh = b + a * h
    edge = hh[0:1, :] if reverse else hh[SUBLANES - 1:SUBLANES, :]
    return hh, jnp.broadcast_to(edge, hh.shape)


def _rnn_kernel(*refs, T, D, reverse, final, write_h):
    uc_ref, wg_ref, br_ref, bi_ref, lam_ref, h0_ref = refs[:6]
    pos = 6
    if final:
        hf_ref, gg_ref, ga_ref, yb_ref, x_ref, mod_ref, wa_ref, wo_ref = refs[pos:pos + 8]
        pos += 8
    outs = []
    if write_h:
        outs.append(refs[pos]); pos += 1
    st_ref = refs[pos]; pos += 1
    if final:
        xo_ref = refs[pos]; pos += 1
    a_buf, b_buf, hcar = refs[pos:pos + 3]

    t = pl.program_id(1)

    @pl.when(t == 0)
    def _():
        hcar[...] = jnp.broadcast_to(h0_ref[0], (SUBLANES, D))

    z = -lam_ref[...]
    clam = -LRU_C * (jnp.maximum(z, 0.0) + jnp.log1p(jnp.exp(-jnp.abs(z))))
    GW = GATE_GROUP
    for g in range(D // GW):
        sl = slice(g * GW, (g + 1) * GW)
        ug = uc_ref[0, :, sl]
        pre = jnp.dot(ug, wg_ref[g], preferred_element_type=F32)
        r = _sigmoid(pre[:, :GW] + br_ref[:, sl])
        gi = _sigmoid(pre[:, GW:] + bi_ref[:, sl])
        a = jnp.exp(r * clam[:, sl])
        a_buf[:, sl] = a
        b_buf[:, sl] = jnp.sqrt(1.0 - a * a) * (gi * ug.astype(F32))

    n_blk = T // SUBLANES

    def body(i, h):
        blk = (n_blk - 1 - i) if reverse else i
        rows = pl.ds(pl.multiple_of(blk * SUBLANES, SUBLANES), SUBLANES)
        hh, h_next = _scan_rows(a_buf[rows, :], b_buf[rows, :], h, reverse)
        b_buf[rows, :] = hh
        return h_next

    h_end = lax.fori_loop(0, n_blk, body, hcar[...])
    hcar[...] = h_end
    st_ref[0] = h_end[0:1, :]
    if write_h:
        outs[0][0] = b_buf[...].astype(BF16)
    if final:
        y = ((hf_ref[0].astype(F32) + b_buf[...]) * gg_ref[0].astype(F32)).astype(BF16)
        ya = jnp.dot(y, wa_ref[...], preferred_element_type=F32)
        m = (ga_ref[0].astype(F32) * ya + yb_ref[0].astype(F32)).astype(BF16)
        out = jnp.dot(m, wo_ref[...], preferred_element_type=F32)
        xo_ref[0] = x_ref[0] + mod_ref[0, 2:3, :] * out


def _rnn_call(uc, wg, br, bi, lam, h0, T, *, reverse, final_args=None, write_h=True):
    B, L, D = uc.shape
    nt = L // T
    final = final_args is not None
    tmap = (lambda b, t: (b, nt - 1 - t, 0)) if reverse else (lambda b, t: (b, t, 0))
    tile = pl.BlockSpec((1, T, D), tmap)
    per_batch_h0 = h0.shape[0] > 1
    in_specs = [
        tile,
        _const_spec(wg.shape),
        _const_spec((1, D)),
        _const_spec((1, D)),
        _const_spec((1, D)),
        pl.BlockSpec((1, 1, D), (lambda b, t: (b, 0, 0)) if per_batch_h0 else (lambda b, t: (0, 0, 0))),
    ]
    args = [uc, wg, br, bi, lam, h0]
    out_specs, out_shape = [], []
    if write_h:
        out_specs.append(tile)
        out_shape.append(jax.ShapeDtypeStruct((B, L, D), BF16))
    out_specs.append(pl.BlockSpec((1, 1, D), lambda b, t: (b, 0, 0)))
    out_shape.append(jax.ShapeDtypeStruct((B, 1, D), F32))
    if final:
        hf, gg, ga, yb, x, mod, wa, wo = final_args
        per_batch = mod.shape[0] > 1
        in_specs += [tile, tile, tile, tile, tile,
                     pl.BlockSpec((1, 6, D), (lambda b, t: (b, 0, 0)) if per_batch else (lambda b, t: (0, 0, 0))),
                     _const_spec((D, D)), _const_spec((D, D))]
        args += [hf, gg, ga, yb, x, mod, wa, wo]
        out_specs.append(tile)
        out_shape.append(jax.ShapeDtypeStruct((B, L, D), F32))
    return pl.pallas_call(
        functools.partial(_rnn_kernel, T=T, D=D, reverse=reverse, final=final, write_h=write_h),
        grid=(B, nt),
        in_specs=in_specs,
        out_specs=out_specs,
        out_shape=out_shape,
        scratch_shapes=[pltpu.VMEM((T, D), F32), pltpu.VMEM((T, D), F32), pltpu.VMEM((SUBLANES, D), F32)],
        compiler_params=_params(2),
        name="rnn_bwd_out" if final else ("rnn_bwd" if reverse else "rnn_fwd"),
    )(*args)


def _swiglu_rows(hb, w1_ref, w3_ref, w2_ref, FC):
    F = w1_ref.shape[-1]
    acc = None
    for c in range(F // FC):
        sl = slice(c * FC, (c + 1) * FC)
        a = jnp.dot(hb, w1_ref[:, sl], preferred_element_type=F32)
        b = jnp.dot(hb, w3_ref[:, sl], preferred_element_type=F32)
        act = (a * _sigmoid(a) * b).astype(BF16)
        part = jnp.dot(act, w2_ref[sl, :], preferred_element_type=F32)
        acc = part if acc is None else acc + part
    return acc


def _ffn_kernel(x_ref, mod_ref, g_ref, w1_ref, w3_ref, w2_ref, *rest, FC, final_norm):
    if final_norm:
        fg_ref, o_ref = rest
    else:
        (o_ref,) = rest
    x = x_ref[0]
    hb = _norm_mod(x, g_ref[...], mod_ref[0, 3:4, :], mod_ref[0, 4:5, :]).astype(BF16)
    y = x + mod_ref[0, 5:6, :] * _swiglu_rows(hb, w1_ref, w3_ref, w2_ref, FC)
    if final_norm:
        y = y * lax.rsqrt(jnp.mean(y * y, axis=-1, keepdims=True) + EPS) * fg_ref[...]
    o_ref[0] = y


def _ffn_call(x, mod, g2, w1, w3, w2, T, final_g=None):
    B, L, D = x.shape
    F = w1.shape[-1]
    per_batch = mod.shape[0] > 1
    tile = pl.BlockSpec((1, T, D), lambda b, t: (b, t, 0))
    in_specs = [tile,
                pl.BlockSpec((1, 6, D), (lambda b, t: (b, 0, 0)) if per_batch else (lambda b, t: (0, 0, 0))),
                _const_spec((1, D)), _const_spec((D, F)), _const_spec((D, F)), _const_spec((F, D))]
    args = [x, mod, g2, w1, w3, w2]
    if final_g is not None:
        in_specs.append(_const_spec((1, D)))
        args.append(final_g)
    return pl.pallas_call(
        functools.partial(_ffn_kernel, FC=_ff_chunk(F), final_norm=final_g is not None),
        grid=(B, L // T),
        in_specs=in_specs,
        out_specs=tile,
        out_shape=jax.ShapeDtypeStruct((B, L, D), F32),
        compiler_params=_params(2),
        name="dense_swiglu",
    )(*args)


def _ff_chunk(F):
    for fc in (512, 256, 128):
        if F % fc == 0:
            return fc
    return F


def _route_kernel(x_ref, mod_ref, g_ref, rw_ref, hb_ref, pos_ref, post_ref, comb_ref, cnt_ref, *, T, E):
    h = _norm_mod(x_ref[...], g_ref[...], mod_ref[0, 3:4, :], mod_ref[0, 4:5, :])
    hb_ref[...] = h.astype(BF16)
    logits = jnp.dot(h, rw_ref[...], precision=lax.Precision.HIGHEST, preferred_element_type=F32)
    lane = lax.broadcasted_iota(jnp.int32, logits.shape, 1).astype(F32)
    neg = jnp.float32(-jnp.inf)
    lg = jnp.where(lane < E, logits, neg)
    m1 = jnp.max(lg, axis=-1, keepdims=True)
    i1 = jnp.min(jnp.where(lg == m1, lane, float(LANES)), axis=-1, keepdims=True)
    lg2 = jnp.where(lane == i1, neg, lg)
    m2 = jnp.max(lg2, axis=-1, keepdims=True)
    i2 = jnp.min(jnp.where(lg2 == m2, lane, float(LANES)), axis=-1, keepdims=True)
    e2 = jnp.exp(m2 - m1)
    den = 1.0 + e2
    sel1 = lane == i1
    sel2 = lane == i2
    comb_ref[...] = jnp.where(sel1, 1.0 / den, jnp.where(sel2, e2 / den, 0.0))
    sel = jnp.where(sel1, 1.0, jnp.where(sel2, 1.0, 0.0))
    row = lax.broadcasted_iota(jnp.int32, (T, T), 0)
    col = lax.broadcasted_iota(jnp.int32, (T, T), 1)
    tri = jnp.where(col < row, 1.0, 0.0).astype(BF16)
    rank = jnp.dot(tri, sel.astype(BF16), preferred_element_type=F32)
    pos = jnp.where(sel > 0.0, rank, -1.0)
    pos_ref[...] = pos
    post_ref[0] = jnp.transpose(pos)[0:SUBLANES, :]
    cnt_ref[0] = jnp.sum(sel, axis=0, keepdims=True)


def _route_call(x2, mod, g2, rw, T, tiles_per_batch):
    N, D = x2.shape
    E = rw.shape[1]
    assert E <= SUBLANES
    nT = N // T
    rw_pad = jnp.zeros((D, LANES), F32).at[:, :E].set(rw)
    per_batch = mod.shape[0] > 1
    tile = pl.BlockSpec((T, D), lambda j: (j, 0))
    col = pl.BlockSpec((T, LANES), lambda j: (j, 0))
    return pl.pallas_call(
        functools.partial(_route_kernel, T=T, E=E),
        grid=(nT,),
        in_specs=[tile,
                  pl.BlockSpec((1, 6, D), (lambda j: (j // tiles_per_batch, 0, 0)) if per_batch else (lambda j: (0, 0, 0))),
                  _const_spec((1, D)), _const_spec((D, LANES))],
        out_specs=[tile, col, pl.BlockSpec((1, SUBLANES, T), lambda j: (j, 0, 0)), col,
                   pl.BlockSpec((1, 1, LANES), lambda j: (j, 0, 0))],
        out_shape=[jax.ShapeDtypeStruct((N, D), BF16), jax.ShapeDtypeStruct((N, LANES), F32),
                   jax.ShapeDtypeStruct((nT, SUBLANES, T), F32), jax.ShapeDtypeStruct((N, LANES), F32),
                   jax.ShapeDtypeStruct((nT, 1, LANES), F32)],
        compiler_params=_params(1),
        name="moe_route",
    )(x2, mod, g2, rw_pad)


def _compact_kernel(cnt_ref, start_ref, hb_ref, post_ref, hs_in_ref, hs_ref, buf, sem, busy, *, T, E, CH):
    del hs_in_ref
    j = pl.program_id(0)
    nj = pl.num_programs(0)

    @pl.when(j == 0)
    def _():
        for e in range(E):
            busy[e] = 0

    def copy(e, row):
        return pltpu.make_async_copy(buf.at[e], hs_ref.at[pl.ds(row, CH), :], sem.at[e])

    hbt = hb_ref[...]
    riota = lax.broadcasted_iota(jnp.int32, (CH, T), 0).astype(F32)
    for e in range(E):
        n = cnt_ref[j * E + e]
        s0 = start_ref[j * E + e]
        pe = post_ref[0, e:e + 1, :]

        def chunk(c, carry, e=e, pe=pe, s0=s0):
            base = c * CH

            @pl.when(busy[e] == 1)
            def _():
                copy(e, 0).wait()

            onehot = jnp.where(pe == riota + base.astype(F32), 1.0, 0.0).astype(BF16)
            buf[e] = jnp.dot(onehot, hbt, preferred_element_type=F32).astype(BF16)
            copy(e, pl.multiple_of(s0 + base, BF16_ROWS)).start()
            busy[e] = 1
            return carry

        lax.fori_loop(0, (n + CH - 1) // CH, chunk, 0)

    @pl.when(j == nj - 1)
    def _():
        for e in range(E):
            @pl.when(busy[e] == 1)
            def _():
                copy(e, 0).wait()


def _compact_call(cnt, start, hb, post, hs0, T, E, CH):
    N, D = hb.shape
    nT = N // T
    grid_spec = pltpu.PrefetchScalarGridSpec(
        num_scalar_prefetch=2,
        grid=(nT,),
        in_specs=[pl.BlockSpec((T, D), lambda j, *_: (j, 0)),
                  pl.BlockSpec((1, SUBLANES, T), lambda j, *_: (j, 0, 0)),
                  pl.BlockSpec(memory_space=pl.ANY)],
        out_specs=pl.BlockSpec(memory_space=pl.ANY),
        scratch_shapes=[pltpu.VMEM((E, CH, D), BF16), pltpu.SemaphoreType.DMA((E,)), pltpu.SMEM((E,), jnp.int32)],
    )
    return pl.pallas_call(
        functools.partial(_compact_kernel, T=T, E=E, CH=CH),
        grid_spec=grid_spec,
        out_shape=jax.ShapeDtypeStruct(hs0.shape, BF16),
        input_output_aliases={4: 0},
        compiler_params=_params(1),
        name="moe_compact",
    )(cnt, start, hb, post, hs0)


def _gffn_kernel(te_ref, valid_ref, hs_ref, w1_ref, w3_ref, w2_ref, ys_ref, *, FC):
    del te_ref
    i = pl.program_id(0)

    @pl.when(valid_ref[i] != 0)
    def _():
        ys_ref[...] = _swiglu_rows(hs_ref[...], w1_ref.at[0], w3_ref.at[0], w2_ref.at[0], FC).astype(BF16)

    @pl.when(valid_ref[i] == 0)
    def _():
        ys_ref[...] = jnp.zeros(ys_ref.shape, BF16)


def _gffn_call(tile_expert, valid, hs, w1, w3, w2, TM):
    R, D = hs.shape
    E, _, F = w1.shape
    grid_spec = pltpu.PrefetchScalarGridSpec(
        num_scalar_prefetch=2,
        grid=(R // TM,),
        in_specs=[pl.BlockSpec((TM, D), lambda i, te, va: (i, 0)),
                  pl.BlockSpec((1, D, F), lambda i, te, va: (te[i], 0, 0)),
                  pl.BlockSpec((1, D, F), lambda i, te, va: (te[i], 0, 0)),
                  pl.BlockSpec((1, F, D), lambda i, te, va: (te[i], 0, 0))],
        out_specs=pl.BlockSpec((TM, D), lambda i, te, va: (i, 0)),
    )
    return pl.pallas_call(
        functools.partial(_gffn_kernel, FC=_ff_chunk(F)),
        grid_spec=grid_spec,
        out_shape=jax.ShapeDtypeStruct((R, D), BF16),
        compiler_params=_params(1),
        name="moe_grouped_swiglu",
    )(tile_expert, valid, hs, w1, w3, w2)


def _combine_kernel(cnt_ref, start_ref, x_ref, mod_ref, pos_ref, comb_ref, ys_ref, *rest, T, E, CH, final_norm):
    if final_norm:
        fg_ref, o_ref, ybuf, xbuf, acc, sem, xsem = rest
    else:
        o_ref, ybuf, xbuf, acc, sem, xsem = rest
    j = pl.program_id(0)
    nj = pl.num_programs(0)
    slot = j % 2

    def first_chunk(jj, sl, e):
        row = pl.multiple_of(start_ref[jj * E + e], BF16_ROWS)
        return pltpu.make_async_copy(ys_ref.at[pl.ds(row, CH), :], ybuf.at[sl, e], sem.at[sl, e])

    @pl.when(j == 0)
    def _():
        for e in range(E):
            first_chunk(0, 0, e).start()

    @pl.when(j + 1 < nj)
    def _():
        for e in range(E):
            first_chunk(j + 1, 1 - slot, e).start()

    liota = lax.broadcasted_iota(jnp.int32, (T, CH), 1).astype(F32)
    for e in range(E):
        n = cnt_ref[j * E + e]
        s0 = start_ref[j * E + e]
        pe = pos_ref[:, e:e + 1]
        we = comb_ref[:, e:e + 1]
        first_chunk(j, slot, e).wait()
        onehot = jnp.where(pe == liota, 1.0, 0.0).astype(BF16)
        part = we * jnp.dot(onehot, ybuf[slot, e], preferred_element_type=F32)
        if e == 0:
            acc[...] = part
        else:
            acc[...] += part

        def extra(c, carry, pe=pe, we=we, s0=s0):
            base = c * CH
            cp = pltpu.make_async_copy(ys_ref.at[pl.ds(pl.multiple_of(s0 + base, BF16_ROWS), CH), :], xbuf, xsem)
            cp.start()
            cp.wait()
            onehot_c = jnp.where(pe == liota + base.astype(F32), 1.0, 0.0).astype(BF16)
            acc[...] += we * jnp.dot(onehot_c, xbuf[...], preferred_element_type=F32)
            return carry

        lax.fori_loop(1, (n + CH - 1) // CH, extra, 0)
    y = x_ref[...] + mod_ref[0, 5:6, :] * acc[...]
    if final_norm:
        y = y * lax.rsqrt(jnp.mean(y * y, axis=-1, keepdims=True) + EPS) * fg_ref[...]
    o_ref[...] = y


def _combine_call(cnt, start, x2, mod, pos, comb, ys, T, E, CH, tiles_per_batch, final_g=None):
    N, D = x2.shape
    per_batch = mod.shape[0] > 1
    tile = pl.BlockSpec((T, D), lambda j, *_: (j, 0))
    col = pl.BlockSpec((T, LANES), lambda j, *_: (j, 0))
    in_specs = [tile,
                pl.BlockSpec((1, 6, D), (lambda j, *_: (j // tiles_per_batch, 0, 0)) if per_batch
                             else (lambda j, *_: (0, 0, 0))),
                col, col, pl.BlockSpec(memory_space=pl.ANY)]
    args = [cnt, start, x2, mod, pos, comb, ys]
    if final_g is not None:
        in_specs.append(pl.BlockSpec((1, D), lambda j, *_: (0, 0)))
        args.append(final_g)
    grid_spec = pltpu.PrefetchScalarGridSpec(
        num_scalar_prefetch=2,
        grid=(N // T,),
        in_specs=in_specs,
        out_specs=tile,
        scratch_shapes=[pltpu.VMEM((2, E, CH, D), BF16), pltpu.VMEM((CH, D), BF16), pltpu.VMEM((T, D), F32),
                        pltpu.SemaphoreType.DMA((2, E)), pltpu.SemaphoreType.DMA(())],
    )
    return pl.pallas_call(
        functools.partial(_combine_kernel, T=T, E=E, CH=CH, final_norm=final_g is not None),
        grid_spec=grid_spec,
        out_shape=jax.ShapeDtypeStruct((N, D), F32),
        compiler_params=_params(1),
        name="moe_combine",
    )(*args)


def _moe(x, mod, g2, rw, w1, w3, w2, final_g=None):
    B, L, D = x.shape
    N = B * L
    E = rw.shape[1]
    T = min(512, L)
    CH = min(256, T)
    TM = 512
    nT = N // T
    x2 = x.reshape(N, D)
    hb, pos, post, comb, cnt_f = _route_call(x2, mod, g2, rw, T, L // T)

    cnt = cnt_f[:, 0, :E].astype(jnp.int32)
    npad = (cnt + BF16_ROWS - 1) // BF16_ROWS * BF16_ROWS
    glen = (npad.sum(0) + CH + TM - 1) // TM * TM
    gend = jnp.cumsum(glen)
    start = (gend - glen)[None, :] + jnp.cumsum(npad, axis=0) - npad
    rows_bound = 2 * N + nT * E * BF16_ROWS + E * (CH + TM)
    R = (rows_bound + TM - 1) // TM * TM
    tile_row = jnp.arange(R // TM, dtype=jnp.int32) * TM
    tile_expert = jnp.minimum(jnp.sum(tile_row[:, None] >= gend[None, :], axis=1), E - 1).astype(jnp.int32)
    valid = (tile_row < gend[-1]).astype(jnp.int32)
    cnt1, start1 = cnt.reshape(-1), start.reshape(-1).astype(jnp.int32)

    hs = _compact_call(cnt1, start1, hb, post, jnp.zeros((R, D), BF16), T, E, CH)
    ys = _gffn_call(tile_expert, valid, hs, w1, w3, w2, TM)
    out = _combine_call(cnt1, start1, x2, mod, pos, comb, ys, T, E, CH, L // T, final_g)
    return out.reshape(B, L, D)


def _pack_gates(wr, wi):
    H, hd, _ = wr.shape
    per = GATE_GROUP // hd
    G = H // per

    def blockdiag(w):
        w = w.reshape(G, per, hd, hd)
        eye = jnp.eye(per, dtype=w.dtype)
        return jnp.einsum('gpij,pq->gpiqj', w, eye).reshape(G, GATE_GROUP, GATE_GROUP)

    return jnp.concatenate([blockdiag(wr), blockdiag(wi)], axis=-1).astype(BF16)


def _mixer(x, mod, p, stream_tiles, conv_stride, h0_f, h0_b, full):
    T_in, T_rnn, T_cv, HB = stream_tiles
    uc, gg, v, ga, gb = _inproj_call(x, mod, p['g1'], p['w_in'], p['rnn_cw'], p['rnn_cb'], T_in)
    rnn_f = (uc, p['wg'][0], p['br'][0], p['bi'][0], p['lam'][0], h0_f, T_rnn)
    rnn_b = (uc, p['wg'][1], p['br'][1], p['bi'][1], p['lam'][1], h0_b, T_rnn)
    if not full:
        (s_f,) = _rnn_call(*rnn_f, reverse=False, write_h=False)
        (s_b,) = _rnn_call(*rnn_b, reverse=True, write_h=False)
        return None, s_f, s_b
    ybg = _convb_call(v, gb, p['cw'], p['cb'], p['lg'], p['lb'], p['w_b'], T_cv, HB, conv_stride)
    hf, s_f = _rnn_call(*rnn_f, reverse=False)
    s_b, x_new = _rnn_call(*rnn_b, reverse=True, write_h=False,
                           final_args=(hf, gg, ga, ybg, x, mod, p['w_a'], p['w_out']))
    return x_new, s_f, s_b


def kernel(x, c, ctx, c_ctx, mod_w, mod_b, norm1_g, norm2_g, w_in, rnn_conv_w, rnn_conv_b, lru_wr, lru_br, lru_wi, lru_bi, lru_lam, conv_w, conv_b, conv_ln_g, conv_ln_b, w_branch_a, w_branch_b, w_out, ffn_w1, ffn_w3, ffn_w2, moe_router, moe_w1, moe_w3, moe_w2, final_g):
    B, L, D = x.shape
    Lc = ctx.shape[1]
    depth = mod_w.shape[0]
    assert B + 1 <= SUBLANES and D % GATE_GROUP == 0 and GATE_GROUP % lru_wr.shape[-1] == 0

    rows = jnp.zeros((SUBLANES, D), F32).at[:B].set(c).at[B].set(c_ctx)
    mod_all = _mod_call(rows, mod_w, mod_b)

    lat_tiles = (min(512, L), min(512, L), min(1024, L), min(1024, L))
    ctx_tiles = (Lc, Lc, Lc, BF16_ROWS)
    T_ffn = min(512, L)
    zero_state = jnp.zeros((1, 1, D), F32)
    fg = final_g.reshape(1, D)

    for i in range(depth):
        last = i == depth - 1
        mod_x = mod_all[i, :B].reshape(B, 6, D)
        mod_c = mod_all[i, B].reshape(1, 6, D)
        p = dict(
            g1=norm1_g[i].reshape(1, D), w_in=w_in[i].astype(BF16),
            rnn_cw=rnn_conv_w[i], rnn_cb=rnn_conv_b[i].reshape(1, D),
            wg=[_pack_gates(lru_wr[i, d], lru_wi[i, d]) for d in range(2)],
            br=[lru_br[i, d].reshape(1, D) for d in range(2)], bi=[lru_bi[i, d].reshape(1, D) for d in range(2)],
            lam=[lru_lam[i, d].reshape(1, D) for d in range(2)],
            cw=conv_w[i], cb=conv_b[i].reshape(1, D), lg=conv_ln_g[i].reshape(1, D), lb=conv_ln_b[i].reshape(1, D),
            w_a=w_branch_a[i].astype(BF16), w_b=w_branch_b[i].astype(BF16), w_out=w_out[i].astype(BF16))
        g2 = norm2_g[i].reshape(1, D)
        j = i // 2
        dense = i % 2 == 0
        if dense:
            ffn_w = (ffn_w1[j].astype(BF16), ffn_w3[j].astype(BF16), ffn_w2[j].astype(BF16))
        else:
            ffn_w = (moe_router[j], moe_w1[j].astype(BF16), moe_w3[j].astype(BF16), moe_w2[j].astype(BF16))

        ctx_new, s_f, s_b = _mixer(ctx, mod_c, p, ctx_tiles, 1, zero_state, zero_state, full=not last)
        if not last:
            ctx = (_ffn_call(ctx_new, mod_c, g2, *ffn_w, Lc) if dense
                   else _moe(ctx_new, mod_c, g2, *ffn_w))
        x, _, _ = _mixer(x, mod_x, p, lat_tiles, GRID_W, s_f, s_b, full=True)
        fgi = fg if last else None
        x = (_ffn_call(x, mod_x, g2, *ffn_w, T_ffn, fgi) if dense else _moe(x, mod_x, g2, *ffn_w, fgi))
    return x
```

```python
import functools

import jax
import jax.numpy as jnp
from jax import lax
from jax.experimental import pallas as pl
from jax.experimental.pallas import tpu as pltpu

EPS = 1e-6
LRU_C = 8.0
GRID_W = 64
RNN_PAD_L = 2
GATE_GROUP = 256
SUBLANES = 8
LANES = 128
BF16_ROWS = 16
SCAN_SUB = 128
VMEM_LIMIT = 56 * 1024 * 1024

F32 = jnp.float32
BF16 = jnp.bfloat16


def _sigmoid(x):
    return 0.5 * jnp.tanh(0.5 * x) + 0.5


def _gelu_tanh(x):
    return 0.5 * x * (1.0 + jnp.tanh(0.7978845608028654 * (x + 0.044715 * (x * x * x))))


def _norm_mod(x, g, shift, scale):
    ms = jnp.mean(x * x, axis=-1, keepdims=True)
    y = x * lax.rsqrt(ms + EPS) * g
    return y * (1.0 + scale) + shift


def _params(n_axes):
    return pltpu.CompilerParams(dimension_semantics=("arbitrary",) * n_axes, vmem_limit_bytes=VMEM_LIMIT)


def _const_spec(shape):
    nd = len(shape)
    return pl.BlockSpec(shape, lambda *_: (0,) * nd, pipeline_mode=pl.Buffered(1))


def _gate_matmuls(read_u, wg_ref, D):
    GW = GATE_GROUP
    return [jnp.dot(read_u(slice(g * GW, (g + 1) * GW)), wg_ref[g], preferred_element_type=F32)
            for g in range(D // GW)]


def _lru_coefficients(read_u, wg_ref, br_ref, bi_ref, lam_ref, a_buf, b_buf, D, pre_all=None):
    z = -lam_ref[...]
    clam = -LRU_C * (jnp.maximum(z, 0.0) + jnp.log1p(jnp.exp(-jnp.abs(z))))
    GW = GATE_GROUP
    if pre_all is None:
        pre_all = _gate_matmuls(read_u, wg_ref, D)
    for g in range(D // GW):
        sl = slice(g * GW, (g + 1) * GW)
        ug = read_u(sl)
        pre = pre_all[g]
        r = _sigmoid(pre[:, :GW] + br_ref[:, sl])
        gi = _sigmoid(pre[:, GW:] + bi_ref[:, sl])
        a = jnp.exp(r * clam[:, sl])
        a_buf[:, sl] = a
        q = 1.0 - a * a
        root = jnp.where(q > 0.0, q * lax.rsqrt(q), 0.0)
        b_buf[:, sl] = root * (gi * ug.astype(F32))


def _scan_rows(a, b, h, reverse):
    sub = lax.broadcasted_iota(jnp.int32, a.shape, 0)
    for k in (1, 2, 4):
        if reverse:
            ar, br, m = pltpu.roll(a, SUBLANES - k, 0), pltpu.roll(b, SUBLANES - k, 0), sub < SUBLANES - k
        else:
            ar, br, m = pltpu.roll(a, k, 0), pltpu.roll(b, k, 0), sub >= k
        b = jnp.where(m, a * br + b, b)
        a = jnp.where(m, a * ar, a)
    hh = b + a * h
    edge = hh[0:1, :] if reverse else hh[SUBLANES - 1:SUBLANES, :]
    return hh, jnp.broadcast_to(edge, hh.shape)


def _scan_interleaved(a_buf, b_buf, h, T, reverse, mxu_work):
    n_blk = T // SUBLANES
    mxu_work(0)
    for i in range(n_blk):
        r0 = (n_blk - 1 - i if reverse else i) * SUBLANES
        hh, h = _scan_rows(a_buf[r0:r0 + SUBLANES, :], b_buf[r0:r0 + SUBLANES, :], h, reverse)
        b_buf[r0:r0 + SUBLANES, :] = hh
        mxu_work(i + 1)
    return h


def _mod_kernel(rows_ref, w_ref, b_ref, o_ref):
    r = rows_ref[...]
    s = r * _sigmoid(r)
    o_ref[0] = jnp.dot(s, w_ref[0], precision=lax.Precision.HIGHEST, preferred_element_type=F32) + b_ref[0]


def _mod_call(rows, mod_w, mod_b):
    depth, d, n6 = mod_w.shape
    tn = d
    return pl.pallas_call(
        _mod_kernel,
        grid=(depth, n6 // tn),
        in_specs=[
            pl.BlockSpec((SUBLANES, d), lambda l, n: (0, 0)),
            pl.BlockSpec((1, d, tn), lambda l, n: (l, 0, n)),
            pl.BlockSpec((1, 1, tn), lambda l, n: (l, 0, n)),
        ],
        out_specs=pl.BlockSpec((1, SUBLANES, tn), lambda l, n: (l, 0, n)),
        out_shape=jax.ShapeDtypeStruct((depth, SUBLANES, n6), F32),
        compiler_params=_params(2),
        name="adaln_mod",
    )(rows, mod_w, mod_b.reshape(depth, 1, n6))


def _inproj_kernel(x_ref, xp_ref, xn_ref, mod_ref, g_ref, w_ref, cw_ref, cb_ref,
                   wg_ref, br_ref, bi_ref, lam_ref, h0_ref,
                   uc_ref, gg_ref, v_ref, ga_ref, gb_ref, hf_ref, st_ref,
                   zbuf, hbuf, a_buf, b_buf, hcar, *, T, D, KW):
    t = pl.program_id(1)
    nt = pl.num_programs(1)

    @pl.when(t == 0)
    def _():
        hcar[...] = jnp.broadcast_to(h0_ref[0], (SUBLANES, D))

    shift = mod_ref[0, 0:1, :]
    scale = mod_ref[0, 1:2, :]
    g = g_ref[...]
    h = _norm_mod(x_ref[0], g, shift, scale).astype(BF16)
    hp = _norm_mod(xp_ref[0], g, shift, scale).astype(BF16)
    hn = _norm_mod(xn_ref[0], g, shift, scale).astype(BF16)

    w0 = w_ref[:, 0:D]
    zbuf[0:SUBLANES, :] = jnp.dot(hp, w0, preferred_element_type=F32) * (t > 0).astype(F32)
    zbuf[SUBLANES:SUBLANES + T, :] = jnp.dot(h, w0, preferred_element_type=F32)
    zbuf[SUBLANES + T:2 * SUBLANES + T, :] = jnp.dot(hn, w0, preferred_element_type=F32) * (t < nt - 1).astype(F32)
    uc = jnp.broadcast_to(cb_ref[...], (T, D))
    for k in range(KW):
        uc = uc + cw_ref[k:k + 1, :] * zbuf[SUBLANES - RNN_PAD_L + k:SUBLANES - RNN_PAD_L + k + T, :]
    uc_ref[0] = uc.astype(BF16)

    hbuf[...] = h
    SUB = a_buf.shape[1]
    n_sub = T // SUB

    def sub_rows(s):
        return pl.ds(s * SUB if isinstance(s, int) else pl.multiple_of(s * SUB, SUB), SUB)

    def coefficients(s, pre_all=None):
        rows = sub_rows(s)
        _lru_coefficients(lambda sl: uc_ref[0, rows, sl], wg_ref, br_ref, bi_ref, lam_ref,
                          a_buf.at[s % 2], b_buf.at[s % 2], D, pre_all)

    def sub_tile(s, hc, prepare_next):
        rows = sub_rows(s)
        if prepare_next:
            nxt = sub_rows(s + 1)
            pre_next = _gate_matmuls(lambda sl: uc_ref[0, nxt, sl], wg_ref, D)
        hs = hbuf[rows, :]

        def proj(g):
            return jnp.dot(hs, w_ref[:, g * D:(g + 1) * D], preferred_element_type=F32)

        gg_ref[0, rows, :] = _gelu_tanh(proj(1)).astype(BF16)
        v_ref[0, rows, :] = (proj(2) * _sigmoid(proj(3))).astype(BF16)
        ga_ref[0, rows, :] = _sigmoid(proj(4)).astype(BF16)
        gb_ref[0, rows, :] = _sigmoid(proj(5)).astype(BF16)
        a_s, b_s = a_buf.at[s % 2], b_buf.at[s % 2]
        for k in range(SUB // SUBLANES):
            blk = slice(k * SUBLANES, (k + 1) * SUBLANES)
            hh, hc = _scan_rows(a_s[blk, :], b_s[blk, :], hc, False)
            b_s[blk, :] = hh
        hf_ref[0, rows, :] = b_s[...].astype(BF16)
        if prepare_next:
            coefficients(s + 1, pre_next)
        return hc

    coefficients(0)
    h_end = lax.fori_loop(0, n_sub - 1, functools.partial(sub_tile, prepare_next=True), hcar[...])
    h_end = sub_tile(n_sub - 1, h_end, False)
    hcar[...] = h_end
    st_ref[0] = h_end[0:1, :]


def _inproj_call(x, mod, g1, w_in, cw, cb, wg, br, bi, lam, h0, T):
    B, L, D = x.shape
    KW = cw.shape[0]
    nt = L // T
    per_batch = mod.shape[0] > 1
    per_batch_h0 = h0.shape[0] > 1
    tb = T // SUBLANES
    last = L // SUBLANES - 1
    tile = pl.BlockSpec((1, T, D), lambda b, t: (b, t, 0))
    out_sds = jax.ShapeDtypeStruct((B, L, D), BF16)
    return pl.pallas_call(
        functools.partial(_inproj_kernel, T=T, D=D, KW=KW),
        grid=(B, nt),
        in_specs=[
            tile,
            pl.BlockSpec((1, SUBLANES, D), lambda b, t: (b, jnp.maximum(t * tb - 1, 0), 0)),
            pl.BlockSpec((1, SUBLANES, D), lambda b, t: (b, jnp.minimum((t + 1) * tb, last), 0)),
            pl.BlockSpec((1, 6, D), (lambda b, t: (b, 0, 0)) if per_batch else (lambda b, t: (0, 0, 0))),
            _const_spec((1, D)),
            _const_spec(w_in.shape),
            _const_spec((KW, D)),
            _const_spec((1, D)),
            _const_spec(wg.shape),
            _const_spec((1, D)),
            _const_spec((1, D)),
            _const_spec((1, D)),
            pl.BlockSpec((1, 1, D), (lambda b, t: (b, 0, 0)) if per_batch_h0 else (lambda b, t: (0, 0, 0))),
        ],
        out_specs=[tile] * 6 + [pl.BlockSpec((1, 1, D), lambda b, t: (b, 0, 0))],
        out_shape=[out_sds] * 6 + [jax.ShapeDtypeStruct((B, 1, D), F32)],
        scratch_shapes=[pltpu.VMEM((T + 2 * SUBLANES, D), F32), pltpu.VMEM((T, D), BF16),
                        pltpu.VMEM((2, min(SCAN_SUB, T), D), F32), pltpu.VMEM((2, min(SCAN_SUB, T), D), F32),
                        pltpu.VMEM((SUBLANES, D), F32)],
        compiler_params=_params(2),
        name="in_projection_scan",
    )(x, x, x, mod, g1, w_in, cw, cb, wg, br, bi, lam, h0)


def _convb_kernel(v_ref, vp_ref, vn_ref, cw_ref, cb_ref, lg_ref, lb_ref, wb_ref, gmb_ref, o_ref,
                  vbuf, cbuf, ybuf, *, T, D, HB, S, RC, KW):
    t = pl.program_id(1)
    nt = pl.num_programs(1)
    vbuf[0:HB, :] = vp_ref[0].astype(F32) * (t > 0).astype(F32)
    vbuf[HB:HB + T, :] = v_ref[0].astype(F32)
    vbuf[HB + T:2 * HB + T, :] = vn_ref[0].astype(F32) * (t < nt - 1).astype(F32)
    pad = KW // 2

    def norm_act(r0, rows):
        x = cbuf[pl.ds(r0, rows), :]
        mu = jnp.mean(x, axis=-1, keepdims=True)
        xc = x - mu
        var = jnp.mean(xc * xc, axis=-1, keepdims=True)
        y = xc * lax.rsqrt(var + EPS) * lg_ref[...] + lb_ref[...]
        ybuf[pl.ds(r0, rows), :] = (y * _sigmoid(y)).astype(BF16)

    if S % SUBLANES == 0:
        CR = 2 * S

        def project(r0):
            rows = pl.ds(r0, CR)
            yb = jnp.dot(ybuf[rows, :], wb_ref[...], preferred_element_type=F32)
            o_ref[0, rows, :] = (yb * gmb_ref[0, rows, :].astype(F32)).astype(BF16)

        def conv_rows(r0):
            for lc in range(D // LANES):
                ln = slice(lc * LANES, (lc + 1) * LANES)
                acc0 = jnp.broadcast_to(cb_ref[:, ln], (S, LANES))
                acc1 = acc0
                for p in range(KW + 1):
                    piece = vbuf[pl.ds(r0 + (HB + (p - pad) * S), S), ln]
                    if p < KW:
                        acc0 = acc0 + cw_ref[p:p + 1, ln] * piece
                    if p >= 1:
                        acc1 = acc1 + cw_ref[p - 1:p, ln] * piece
                cbuf[pl.ds(r0, S), ln] = acc0
                cbuf[pl.ds(r0 + S, S), ln] = acc1
            norm_act(r0, CR)

        def chunk(i, carry):
            project(pl.multiple_of((i - 1) * CR, CR))
            conv_rows(pl.multiple_of(i * CR, CR))
            return carry

        conv_rows(0)
        lax.fori_loop(1, T // CR, chunk, 0)
        project(T - CR)
    else:
        for i in range(T // RC):
            acc = jnp.broadcast_to(cb_ref[...], (RC, D))
            for k in range(KW):
                acc = acc + cw_ref[k:k + 1, :] * vbuf[i * RC + HB + (k - pad) * S:i * RC + HB + (k - pad) * S + RC, :]
            cbuf[i * RC:(i + 1) * RC, :] = acc
            norm_act(i * RC, RC)
        yb = jnp.dot(ybuf[...], wb_ref[...], preferred_element_type=F32)
        o_ref[0] = (yb * gmb_ref[0].astype(F32)).astype(BF16)


def _convb_call(v, gmb, cw, cb, lg, lb, wb, T, HB, S):
    B, L, D = v.shape
    KW = cw.shape[0]
    assert (KW // 2) * S <= HB and T % HB == 0 and L % T == 0 and T % (2 * S) == 0
    nt = L // T
    r = T // HB
    last = L // HB - 1
    tile = pl.BlockSpec((1, T, D), lambda b, t: (b, t, 0))
    return pl.pallas_call(
        functools.partial(_convb_kernel, T=T, D=D, HB=HB, S=S, RC=BF16_ROWS, KW=KW),
        grid=(B, nt),
        in_specs=[
            tile,
            pl.BlockSpec((1, HB, D), lambda b, t: (b, jnp.maximum(t * r - 1, 0), 0)),
            pl.BlockSpec((1, HB, D), lambda b, t: (b, jnp.minimum((t + 1) * r, last), 0)),
            _const_spec((KW, D)),
            _const_spec((1, D)),
            _const_spec((1, D)),
            _const_spec((1, D)),
            _const_spec((D, D)),
            tile,
        ],
        out_specs=tile,
        out_shape=jax.ShapeDtypeStruct((B, L, D), BF16),
        scratch_shapes=[pltpu.VMEM((T + 2 * HB, D), F32), pltpu.VMEM((T, D), F32), pltpu.VMEM((T, D), BF16)],
        compiler_params=_params(2),
        name="conv_branch",
    )(v, v, v, cw, cb, lg, lb, wb, gmb)


def _rnn_kernel(*refs, T, D, reverse, final, write_h):
    uc_ref, wg_ref, br_ref, bi_ref, lam_ref, h0_ref = refs[:6]
    pos = 6
    if final:
        hf_ref, gg_ref, ga_ref, yb_ref, x_ref, mod_ref, wa_ref, wo_ref = refs[pos:pos + 8]
        pos += 8
    outs = []
    if write_h:
        outs.append(refs[pos]); pos += 1
    st_ref = refs[pos]; pos += 1
    if final:
        xo_ref = refs[pos]; pos += 1
    a_buf, b_buf, hcar = refs[pos:pos + 3]

    t = pl.program_id(1)

    @pl.when(t == 0)
    def _():
        hcar[...] = jnp.broadcast_to(h0_ref[0], (SUBLANES, D))

    _lru_coefficients(lambda sl: uc_ref[0, :, sl], wg_ref, br_ref, bi_ref, lam_ref, a_buf, b_buf, D)

    n_blk = T // SUBLANES

    def body(i, h):
        blk = (n_blk - 1 - i) if reverse else i
        rows = pl.ds(pl.multiple_of(blk * SUBLANES, SUBLANES), SUBLANES)
        hh, h_next = _scan_rows(a_buf[rows, :], b_buf[rows, :], h, reverse)
        b_buf[rows, :] = hh
        return h_next

    SUB = min(SCAN_SUB, T)

    def project(n_done):
        rows_done = n_done * SUBLANES
        if rows_done == 0 or rows_done % SUB:
            return
        rows = slice(T - rows_done, T - rows_done + SUB) if reverse else slice(rows_done - SUB, rows_done)
        y = ((hf_ref[0, rows, :].astype(F32) + b_buf[rows, :]) * gg_ref[0, rows, :].astype(F32)).astype(BF16)
        ya = jnp.dot(y, wa_ref[...], preferred_element_type=F32)
        m = (ga_ref[0, rows, :].astype(F32) * ya + yb_ref[0, rows, :].astype(F32)).astype(BF16)
        out = jnp.dot(m, wo_ref[...], preferred_element_type=F32)
        xo_ref[0, rows, :] = x_ref[0, rows, :] + mod_ref[0, 2:3, :] * out

    if final:
        h_end = _scan_interleaved(a_buf, b_buf, hcar[...], T, reverse, project)
    else:
        h_end = lax.fori_loop(0, n_blk, body, hcar[...])
    hcar[...] = h_end
    st_ref[0] = h_end[0:1, :]
    if write_h:
        outs[0][0] = b_buf[...].astype(BF16)


def _rnn_call(uc, wg, br, bi, lam, h0, T, *, reverse, final_args=None, write_h=True):
    B, L, D = uc.shape
    nt = L // T
    final = final_args is not None
    tmap = (lambda b, t: (b, nt - 1 - t, 0)) if reverse else (lambda b, t: (b, t, 0))
    tile = pl.BlockSpec((1, T, D), tmap)
    per_batch_h0 = h0.shape[0] > 1
    in_specs = [
        tile,
        _const_spec(wg.shape),
        _const_spec((1, D)),
        _const_spec((1, D)),
        _const_spec((1, D)),
        pl.BlockSpec((1, 1, D), (lambda b, t: (b, 0, 0)) if per_batch_h0 else (lambda b, t: (0, 0, 0))),
    ]
    args = [uc, wg, br, bi, lam, h0]
    out_specs, out_shape = [], []
    if write_h:
        out_specs.append(tile)
        out_shape.append(jax.ShapeDtypeStruct((B, L, D), BF16))
    out_specs.append(pl.BlockSpec((1, 1, D), lambda b, t: (b, 0, 0)))
    out_shape.append(jax.ShapeDtypeStruct((B, 1, D), F32))
    if final:
        hf, gg, ga, yb, x, mod, wa, wo = final_args
        per_batch = mod.shape[0] > 1
        in_specs += [tile, tile, tile, tile, tile,
                     pl.BlockSpec((1, 6, D), (lambda b, t: (b, 0, 0)) if per_batch else (lambda b, t: (0, 0, 0))),
                     _const_spec((D, D)), _const_spec((D, D))]
        args += [hf, gg, ga, yb, x, mod, wa, wo]
        out_specs.append(tile)
        out_shape.append(jax.ShapeDtypeStruct((B, L, D), F32))
    return pl.pallas_call(
        functools.partial(_rnn_kernel, T=T, D=D, reverse=reverse, final=final, write_h=write_h),
        grid=(B, nt),
        in_specs=in_specs,
        out_specs=out_specs,
        out_shape=out_shape,
        scratch_shapes=[pltpu.VMEM((T, D), F32), pltpu.VMEM((T, D), F32), pltpu.VMEM((SUBLANES, D), F32)],
        compiler_params=_params(2),
        name="rnn_bwd_out" if final else ("rnn_bwd" if reverse else "rnn_fwd"),
    )(*args)


def _swiglu_rows(hb, w1_ref, w3_ref, w2_ref, FC):
    F = w1_ref.shape[-1]
    acc = None
    for c in range(F // FC):
        sl = slice(c * FC, (c + 1) * FC)
        a = jnp.dot(hb, w1_ref[:, sl], preferred_element_type=F32)
        b = jnp.dot(hb, w3_ref[:, sl], preferred_element_type=F32)
        act = (a * _sigmoid(a) * b).astype(BF16)
        part = jnp.dot(act, w2_ref[sl, :], preferred_element_type=F32)
        acc = part if acc is None else acc + part
    return acc


def _ffn_kernel(x_ref, mod_ref, g_ref, w1_ref, w3_ref, w2_ref, *rest, FC, final_norm):
    if final_norm:
        fg_ref, o_ref = rest
    else:
        (o_ref,) = rest
    x = x_ref[0]
    hb = _norm_mod(x, g_ref[...], mod_ref[0, 3:4, :], mod_ref[0, 4:5, :]).astype(BF16)
    y = x + mod_ref[0, 5:6, :] * _swiglu_rows(hb, w1_ref, w3_ref, w2_ref, FC)
    if final_norm:
        y = y * lax.rsqrt(jnp.mean(y * y, axis=-1, keepdims=True) + EPS) * fg_ref[...]
    o_ref[0] = y


def _ffn_call(x, mod, g2, w1, w3, w2, T, final_g=None):
    B, L, D = x.shape
    F = w1.shape[-1]
    per_batch = mod.shape[0] > 1
    tile = pl.BlockSpec((1, T, D), lambda b, t: (b, t, 0))
    in_specs = [tile,
                pl.BlockSpec((1, 6, D), (lambda b, t: (b, 0, 0)) if per_batch else (lambda b, t: (0, 0, 0))),
                _const_spec((1, D)), _const_spec((D, F)), _const_spec((D, F)), _const_spec((F, D))]
    args = [x, mod, g2, w1, w3, w2]
    if final_g is not None:
        in_specs.append(_const_spec((1, D)))
        args.append(final_g)
    return pl.pallas_call(
        functools.partial(_ffn_kernel, FC=_ff_chunk(F), final_norm=final_g is not None),
        grid=(B, L // T),
        in_specs=in_specs,
        out_specs=tile,
        out_shape=jax.ShapeDtypeStruct((B, L, D), F32),
        compiler_params=_params(2),
        name="dense_swiglu",
    )(*args)


def _ff_chunk(F):
    for fc in (512, 256, 128):
        if F % fc == 0:
            return fc
    return F


def _route_kernel(x_ref, mod_ref, g_ref, rw_ref, hb_ref, pos_ref, post_ref, comb_ref, cnt_ref, *, T, E):
    h = _norm_mod(x_ref[...], g_ref[...], mod_ref[0, 3:4, :], mod_ref[0, 4:5, :])
    hb_ref[...] = h.astype(BF16)
    logits = jnp.dot(h, rw_ref[...], precision=lax.Precision.HIGHEST, preferred_element_type=F32)
    lane = lax.broadcasted_iota(jnp.int32, logits.shape, 1).astype(F32)
    neg = jnp.float32(-jnp.inf)
    lg = jnp.where(lane < E, logits, neg)
    m1 = jnp.max(lg, axis=-1, keepdims=True)
    i1 = jnp.min(jnp.where(lg == m1, lane, float(LANES)), axis=-1, keepdims=True)
    lg2 = jnp.where(lane == i1, neg, lg)
    m2 = jnp.max(lg2, axis=-1, keepdims=True)
    i2 = jnp.min(jnp.where(lg2 == m2, lane, float(LANES)), axis=-1, keepdims=True)
    e2 = jnp.exp(m2 - m1)
    den = 1.0 + e2
    sel1 = lane == i1
    sel2 = lane == i2
    comb_ref[...] = jnp.where(sel1, 1.0 / den, jnp.where(sel2, e2 / den, 0.0))
    sel = jnp.where(sel1, 1.0, jnp.where(sel2, 1.0, 0.0))
    row = lax.broadcasted_iota(jnp.int32, (T, T), 0)
    col = lax.broadcasted_iota(jnp.int32, (T, T), 1)
    tri = jnp.where(col < row, 1.0, 0.0).astype(BF16)
    rank = jnp.dot(tri, sel.astype(BF16), preferred_element_type=F32)
    pos = jnp.where(sel > 0.0, rank, -1.0)
    pos_ref[...] = pos
    post_ref[0] = jnp.transpose(pos)[0:SUBLANES, :]
    cnt_ref[0] = jnp.sum(sel, axis=0, keepdims=True)


def _route_call(x2, mod, g2, rw, T, tiles_per_batch):
    N, D = x2.shape
    E = rw.shape[1]
    assert E <= SUBLANES
    nT = N // T
    rw_pad = jnp.zeros((D, LANES), F32).at[:, :E].set(rw)
    per_batch = mod.shape[0] > 1
    tile = pl.BlockSpec((T, D), lambda j: (j, 0))
    col = pl.BlockSpec((T, LANES), lambda j: (j, 0))
    return pl.pallas_call(
        functools.partial(_route_kernel, T=T, E=E),
        grid=(nT,),
        in_specs=[tile,
                  pl.BlockSpec((1, 6, D), (lambda j: (j // tiles_per_batch, 0, 0)) if per_batch else (lambda j: (0, 0, 0))),
                  _const_spec((1, D)), _const_spec((D, LANES))],
        out_specs=[tile, col, pl.BlockSpec((1, SUBLANES, T), lambda j: (j, 0, 0)), col,
                   pl.BlockSpec((1, 1, LANES), lambda j: (j, 0, 0))],
        out_shape=[jax.ShapeDtypeStruct((N, D), BF16), jax.ShapeDtypeStruct((N, LANES), F32),
                   jax.ShapeDtypeStruct((nT, SUBLANES, T), F32), jax.ShapeDtypeStruct((N, LANES), F32),
                   jax.ShapeDtypeStruct((nT, 1, LANES), F32)],
        compiler_params=_params(1),
        name="moe_route",
    )(x2, mod, g2, rw_pad)


def _compact_kernel(cnt_ref, start_ref, hb_ref, post_ref, hs_in_ref, hs_ref, buf, sem, busy, *, T, E, CH):
    del hs_in_ref
    j = pl.program_id(0)
    nj = pl.num_programs(0)

    @pl.when(j == 0)
    def _():
        for e in range(E):
            busy[e] = 0

    def copy(e, row):
        return pltpu.make_async_copy(buf.at[e], hs_ref.at[pl.ds(row, CH), :], sem.at[e])

    hbt = hb_ref[...]
    riota = lax.broadcasted_iota(jnp.int32, (CH, T), 0).astype(F32)
    for e in range(E):
        n = cnt_ref[j * E + e]
        s0 = start_ref[j * E + e]
        pe = post_ref[0, e:e + 1, :]

        def chunk(c, carry, e=e, pe=pe, s0=s0):
            base = c * CH

            @pl.when(busy[e] == 1)
            def _():
                copy(e, 0).wait()

            onehot = jnp.where(pe == riota + base.astype(F32), 1.0, 0.0).astype(BF16)
            buf[e] = jnp.dot(onehot, hbt, preferred_element_type=F32).astype(BF16)
            copy(e, pl.multiple_of(s0 + base, BF16_ROWS)).start()
            busy[e] = 1
            return carry

        lax.fori_loop(0, (n + CH - 1) // CH, chunk, 0)

    @pl.when(j == nj - 1)
    def _():
        for e in range(E):
            @pl.when(busy[e] == 1)
            def _():
                copy(e, 0).wait()


def _compact_call(cnt, start, hb, post, hs0, T, E, CH):
    N, D = hb.shape
    nT = N // T
    grid_spec = pltpu.PrefetchScalarGridSpec(
        num_scalar_prefetch=2,
        grid=(nT,),
        in_specs=[pl.BlockSpec((T, D), lambda j, *_: (j, 0)),
                  pl.BlockSpec((1, SUBLANES, T), lambda j, *_: (j, 0, 0)),
                  pl.BlockSpec(memory_space=pl.ANY)],
        out_specs=pl.BlockSpec(memory_space=pl.ANY),
        scratch_shapes=[pltpu.VMEM((E, CH, D), BF16), pltpu.SemaphoreType.DMA((E,)), pltpu.SMEM((E,), jnp.int32)],
    )
    return pl.pallas_call(
        functools.partial(_compact_kernel, T=T, E=E, CH=CH),
        grid_spec=grid_spec,
        out_shape=jax.ShapeDtypeStruct(hs0.shape, BF16),
        input_output_aliases={4: 0},
        compiler_params=_params(1),
        name="moe_compact",
    )(cnt, start, hb, post, hs0)


def _gffn_kernel(te_ref, rows_ref, hs_ref, w1_ref, w3_ref, w2_ref, ys_ref, *, FC, TM):
    del te_ref
    n = rows_ref[pl.program_id(0)]
    half = TM // 2
    w = (w1_ref.at[0], w3_ref.at[0], w2_ref.at[0])

    @pl.when(n > half)
    def _():
        ys_ref[...] = _swiglu_rows(hs_ref[...], *w, FC).astype(BF16)

    @pl.when((n > 0) & (n <= half))
    def _():
        ys_ref[0:half, :] = _swiglu_rows(hs_ref[0:half, :], *w, FC).astype(BF16)
        ys_ref[half:TM, :] = jnp.zeros((TM - half, ys_ref.shape[1]), BF16)

    @pl.when(n == 0)
    def _():
        ys_ref[...] = jnp.zeros(ys_ref.shape, BF16)


def _gffn_call(tile_expert, valid, hs, w1, w3, w2, TM):
    R, D = hs.shape
    E, _, F = w1.shape
    grid_spec = pltpu.PrefetchScalarGridSpec(
        num_scalar_prefetch=2,
        grid=(R // TM,),
        in_specs=[pl.BlockSpec((TM, D), lambda i, te, va: (i, 0)),
                  pl.BlockSpec((1, D, F), lambda i, te, va: (te[i], 0, 0)),
                  pl.BlockSpec((1, D, F), lambda i, te, va: (te[i], 0, 0)),
                  pl.BlockSpec((1, F, D), lambda i, te, va: (te[i], 0, 0))],
        out_specs=pl.BlockSpec((TM, D), lambda i, te, va: (i, 0)),
    )
    return pl.pallas_call(
        functools.partial(_gffn_kernel, FC=_ff_chunk(F), TM=TM),
        grid_spec=grid_spec,
        out_shape=jax.ShapeDtypeStruct((R, D), BF16),
        compiler_params=_params(1),
        name="moe_grouped_swiglu",
    )(tile_expert, valid, hs, w1, w3, w2)


def _combine_kernel(cnt_ref, start_ref, x_ref, mod_ref, pos_ref, comb_ref, ys_ref, *rest, T, E, CH, final_norm):
    if final_norm:
        fg_ref, o_ref, ybuf, xbuf, acc, sem, xsem = rest
    else:
        o_ref, ybuf, xbuf, acc, sem, xsem = rest
    j = pl.program_id(0)
    nj = pl.num_programs(0)
    slot = j % 2

    def first_chunk(jj, sl, e):
        row = pl.multiple_of(start_ref[jj * E + e], BF16_ROWS)
        return pltpu.make_async_copy(ys_ref.at[pl.ds(row, CH), :], ybuf.at[sl, e], sem.at[sl, e])

    @pl.when(j == 0)
    def _():
        for e in range(E):
            first_chunk(0, 0, e).start()

    @pl.when(j + 1 < nj)
    def _():
        for e in range(E):
            first_chunk(j + 1, 1 - slot, e).start()

    liota = lax.broadcasted_iota(jnp.int32, (T, CH), 1).astype(F32)
    for e in range(E):
        n = cnt_ref[j * E + e]
        s0 = start_ref[j * E + e]
        pe = pos_ref[:, e:e + 1]
        we = comb_ref[:, e:e + 1]
        first_chunk(j, slot, e).wait()
        onehot = jnp.where(pe == liota, 1.0, 0.0).astype(BF16)
        part = we * jnp.dot(onehot, ybuf[slot, e], preferred_element_type=F32)
        if e == 0:
            acc[...] = part
        else:
            acc[...] += part

        def extra(c, carry, pe=pe, we=we, s0=s0):
            base = c * CH
            cp = pltpu.make_async_copy(ys_ref.at[pl.ds(pl.multiple_of(s0 + base, BF16_ROWS), CH), :], xbuf, xsem)
            cp.start()
            cp.wait()
            onehot_c = jnp.where(pe == liota + base.astype(F32), 1.0, 0.0).astype(BF16)
            acc[...] += we * jnp.dot(onehot_c, xbuf[...], preferred_element_type=F32)
            return carry

        lax.fori_loop(1, (n + CH - 1) // CH, extra, 0)
    y = x_ref[...] + mod_ref[0, 5:6, :] * acc[...]
    if final_norm:
        y = y * lax.rsqrt(jnp.mean(y * y, axis=-1, keepdims=True) + EPS) * fg_ref[...]
    o_ref[...] = y


def _combine_call(cnt, start, x2, mod, pos, comb, ys, T, E, CH, tiles_per_batch, final_g=None):
    N, D = x2.shape
    per_batch = mod.shape[0] > 1
    tile = pl.BlockSpec((T, D), lambda j, *_: (j, 0))
    col = pl.BlockSpec((T, LANES), lambda j, *_: (j, 0))
    in_specs = [tile,
                pl.BlockSpec((1, 6, D), (lambda j, *_: (j // tiles_per_batch, 0, 0)) if per_batch
                             else (lambda j, *_: (0, 0, 0))),
                col, col, pl.BlockSpec(memory_space=pl.ANY)]
    args = [cnt, start, x2, mod, pos, comb, ys]
    if final_g is not None:
        in_specs.append(pl.BlockSpec((1, D), lambda j, *_: (0, 0)))
        args.append(final_g)
    grid_spec = pltpu.PrefetchScalarGridSpec(
        num_scalar_prefetch=2,
        grid=(N // T,),
        in_specs=in_specs,
        out_specs=tile,
        scratch_shapes=[pltpu.VMEM((2, E, CH, D), BF16), pltpu.VMEM((CH, D), BF16), pltpu.VMEM((T, D), F32),
                        pltpu.SemaphoreType.DMA((2, E)), pltpu.SemaphoreType.DMA(())],
    )
    return pl.pallas_call(
        functools.partial(_combine_kernel, T=T, E=E, CH=CH, final_norm=final_g is not None),
        grid_spec=grid_spec,
        out_shape=jax.ShapeDtypeStruct((N, D), F32),
        compiler_params=_params(1),
        name="moe_combine",
    )(*args)


def _moe(x, mod, g2, rw, w1, w3, w2, final_g=None):
    B, L, D = x.shape
    N = B * L
    E = rw.shape[1]
    T = min(512, L)
    CH = min(256, T)
    TM = 512
    nT = N // T
    x2 = x.reshape(N, D)
    hb, pos, post, comb, cnt_f = _route_call(x2, mod, g2, rw, T, L // T)

    cnt = cnt_f[:, 0, :E].astype(jnp.int32)
    npad = (cnt + BF16_ROWS - 1) // BF16_ROWS * BF16_ROWS
    glen = (npad.sum(0) + CH + TM - 1) // TM * TM
    gend = jnp.cumsum(glen)
    start = (gend - glen)[None, :] + jnp.cumsum(npad, axis=0) - npad
    rows_bound = 2 * N + nT * E * BF16_ROWS + E * (CH + TM)
    R = (rows_bound + TM - 1) // TM * TM
    tile_row = jnp.arange(R // TM, dtype=jnp.int32) * TM
    tile_expert = jnp.minimum(jnp.sum(tile_row[:, None] >= gend[None, :], axis=1), E - 1).astype(jnp.int32)
    data_end = (gend - glen + npad.sum(0))[tile_expert]
    tile_rows = jnp.clip(data_end - tile_row, 0, TM).astype(jnp.int32)
    cnt1, start1 = cnt.reshape(-1), start.reshape(-1).astype(jnp.int32)

    hs = _compact_call(cnt1, start1, hb, post, jnp.zeros((R, D), BF16), T, E, CH)
    ys = _gffn_call(tile_expert, tile_rows, hs, w1, w3, w2, TM)
    out = _combine_call(cnt1, start1, x2, mod, pos, comb, ys, T, E, CH, L // T, final_g)
    return out.reshape(B, L, D)


def _pack_gates(wr, wi):
    H, hd, _ = wr.shape
    per = GATE_GROUP // hd
    G = H // per

    def blockdiag(w):
        w = w.reshape(G, per, hd, hd)
        eye = jnp.eye(per, dtype=w.dtype)
        return jnp.einsum('gpij,pq->gpiqj', w, eye).reshape(G, GATE_GROUP, GATE_GROUP)

    return jnp.concatenate([blockdiag(wr), blockdiag(wi)], axis=-1).astype(BF16)


def _mixer(x, mod, p, stream_tiles, conv_stride, h0_f, h0_b, full):
    T_in, T_rnn, T_cv, HB = stream_tiles
    uc, gg, v, ga, gb, hf, s_f = _inproj_call(x, mod, p['g1'], p['w_in'], p['rnn_cw'], p['rnn_cb'],
                                              p['wg'][0], p['br'][0], p['bi'][0], p['lam'][0], h0_f, T_in)
    rnn_b = (uc, p['wg'][1], p['br'][1], p['bi'][1], p['lam'][1], h0_b, T_rnn)
    if not full:
        (s_b,) = _rnn_call(*rnn_b, reverse=True, write_h=False)
        return None, s_f, s_b
    ybg = _convb_call(v, gb, p['cw'], p['cb'], p['lg'], p['lb'], p['w_b'], T_cv, HB, conv_stride)
    s_b, x_new = _rnn_call(*rnn_b, reverse=True, write_h=False,
                           final_args=(hf, gg, ga, ybg, x, mod, p['w_a'], p['w_out']))
    return x_new, s_f, s_b


def kernel(x, c, ctx, c_ctx, mod_w, mod_b, norm1_g, norm2_g, w_in, rnn_conv_w, rnn_conv_b, lru_wr, lru_br, lru_wi, lru_bi, lru_lam, conv_w, conv_b, conv_ln_g, conv_ln_b, w_branch_a, w_branch_b, w_out, ffn_w1, ffn_w3, ffn_w2, moe_router, moe_w1, moe_w3, moe_w2, final_g):
    B, L, D = x.shape
    Lc = ctx.shape[1]
    depth = mod_w.shape[0]
    assert B + 1 <= SUBLANES and D % GATE_GROUP == 0 and GATE_GROUP % lru_wr.shape[-1] == 0

    rows = jnp.zeros((SUBLANES, D), F32).at[:B].set(c).at[B].set(c_ctx)
    mod_all = _mod_call(rows, mod_w, mod_b)

    lat_tiles = (min(512, L), min(512, L), min(1024, L), min(1024, L))
    ctx_tiles = (Lc, Lc, Lc, BF16_ROWS)
    T_ffn = min(512, L)
    zero_state = jnp.zeros((1, 1, D), F32)
    fg = final_g.reshape(1, D)

    for i in range(depth):
        last = i == depth - 1
        mod_x = mod_all[i, :B].reshape(B, 6, D)
        mod_c = mod_all[i, B].reshape(1, 6, D)
        p = dict(
            g1=norm1_g[i].reshape(1, D), w_in=w_in[i].astype(BF16),
            rnn_cw=rnn_conv_w[i], rnn_cb=rnn_conv_b[i].reshape(1, D),
            wg=[_pack_gates(lru_wr[i, d], lru_wi[i, d]) for d in range(2)],
            br=[lru_br[i, d].reshape(1, D) for d in range(2)], bi=[lru_bi[i, d].reshape(1, D) for d in range(2)],
            lam=[lru_lam[i, d].reshape(1, D) for d in range(2)],
            cw=conv_w[i], cb=conv_b[i].reshape(1, D), lg=conv_ln_g[i].reshape(1, D), lb=conv_ln_b[i].reshape(1, D),
            w_a=w_branch_a[i].astype(BF16), w_b=w_branch_b[i].astype(BF16), w_out=w_out[i].astype(BF16))
        g2 = norm2_g[i].reshape(1, D)
        j = i // 2
        dense = i % 2 == 0
        if dense:
            ffn_w = (ffn_w1[j].astype(BF16), ffn_w3[j].astype(BF16), ffn_w2[j].astype(BF16))
        else:
            ffn_w = (moe_router[j], moe_w1[j].astype(BF16), moe_w3[j].astype(BF16), moe_w2[j].astype(BF16))

        ctx_new, s_f, s_b = _mixer(ctx, mod_c, p, ctx_tiles, 1, zero_state, zero_state, full=not last)
        if not last:
            ctx = (_ffn_call(ctx_new, mod_c, g2, *ffn_w, Lc) if dense
                   else _moe(ctx_new, mod_c, g2, *ffn_w))
        x, _, _ = _mixer(x, mod_x, p, lat_tiles, GRID_W, s_f, s_b, full=True)
        fgi = fg if last else None
        x = (_ffn_call(x, mod_x, g2, *ffn_w, T_ffn, fgi) if dense else _moe(x, mod_x, g2, *ffn_w, fgi))
    return x
```

```python
import functools

import jax
import jax.numpy as jnp
from jax import lax
from jax.experimental import pallas as pl
from jax.experimental.pallas import tpu as pltpu

EPS = 1e-6
LRU_C = 8.0
GRID_W = 64
RNN_PAD_L = 2
GATE_GROUP = 256
SUBLANES = 8
LANES = 128
BF16_ROWS = 16
SCAN_SUB = 128
VMEM_LIMIT = 56 * 1024 * 1024

F32 = jnp.float32
BF16 = jnp.bfloat16


def _sigmoid(x):
    return 0.5 * jnp.tanh(0.5 * x) + 0.5


def _gelu_tanh(x):
    return 0.5 * x * (1.0 + jnp.tanh(0.7978845608028654 * (x + 0.044715 * (x * x * x))))


def _norm_mod(x, g, shift, scale):
    ms = jnp.mean(x * x, axis=-1, keepdims=True)
    y = x * lax.rsqrt(ms + EPS) * g
    return y * (1.0 + scale) + shift


def _params(n_axes):
    return pltpu.CompilerParams(dimension_semantics=("arbitrary",) * n_axes, vmem_limit_bytes=VMEM_LIMIT)


def _const_spec(shape):
    nd = len(shape)
    return pl.BlockSpec(shape, lambda *_: (0,) * nd, pipeline_mode=pl.Buffered(1))


def _lru_coefficients(read_u, wg_ref, br_ref, bi_ref, lam_ref, a_buf, b_buf, D):
    z = -lam_ref[...]
    clam = -LRU_C * (jnp.maximum(z, 0.0) + jnp.log1p(jnp.exp(-jnp.abs(z))))
    GW = GATE_GROUP
    for g in range(D // GW):
        sl = slice(g * GW, (g + 1) * GW)
        ug = read_u(sl)
        pre = jnp.dot(ug, wg_ref[g], preferred_element_type=F32)
        r = _sigmoid(pre[:, :GW] + br_ref[:, sl])
        gi = _sigmoid(pre[:, GW:] + bi_ref[:, sl])
        a = jnp.exp(r * clam[:, sl])
        a_buf[:, sl] = a
        q = 1.0 - a * a
        root = jnp.where(q > 0.0, q * lax.rsqrt(q), 0.0)
        b_buf[:, sl] = root * (gi * ug.astype(F32))


def _scan_rows(a, b, h, reverse):
    sub = lax.broadcasted_iota(jnp.int32, a.shape, 0)
    for k in (1, 2, 4):
        if reverse:
            ar, br, m = pltpu.roll(a, SUBLANES - k, 0), pltpu.roll(b, SUBLANES - k, 0), sub < SUBLANES - k
        else:
            ar, br, m = pltpu.roll(a, k, 0), pltpu.roll(b, k, 0), sub >= k
        b = jnp.where(m, a * br + b, b)
        a = jnp.where(m, a * ar, a)
    hh = b + a * h
    edge = hh[0:1, :] if reverse else hh[SUBLANES - 1:SUBLANES, :]
    return hh, jnp.broadcast_to(edge, hh.shape)


def _scan_interleaved(a_buf, b_buf, h, T, reverse, mxu_work):
    n_blk = T // SUBLANES
    mxu_work(0)
    for i in range(n_blk):
        r0 = (n_blk - 1 - i if reverse else i) * SUBLANES
        hh, h = _scan_rows(a_buf[r0:r0 + SUBLANES, :], b_buf[r0:r0 + SUBLANES, :], h, reverse)
        b_buf[r0:r0 + SUBLANES, :] = hh
        mxu_work(i + 1)
    return h


def _mod_kernel(rows_ref, w_ref, b_ref, o_ref):
    r = rows_ref[...]
    s = r * _sigmoid(r)
    o_ref[0] = jnp.dot(s, w_ref[0], precision=lax.Precision.HIGHEST, preferred_element_type=F32) + b_ref[0]


def _mod_call(rows, mod_w, mod_b):
    depth, d, n6 = mod_w.shape
    tn = d
    return pl.pallas_call(
        _mod_kernel,
        grid=(depth, n6 // tn),
        in_specs=[
            pl.BlockSpec((SUBLANES, d), lambda l, n: (0, 0)),
            pl.BlockSpec((1, d, tn), lambda l, n: (l, 0, n)),
            pl.BlockSpec((1, 1, tn), lambda l, n: (l, 0, n)),
        ],
        out_specs=pl.BlockSpec((1, SUBLANES, tn), lambda l, n: (l, 0, n)),
        out_shape=jax.ShapeDtypeStruct((depth, SUBLANES, n6), F32),
        compiler_params=_params(2),
        name="adaln_mod",
    )(rows, mod_w, mod_b.reshape(depth, 1, n6))


def _inproj_kernel(x_ref, xp_ref, xn_ref, mod_ref, g_ref, w_ref, cw_ref, cb_ref,
                   uc_ref, gg_ref, v_ref, ga_ref, gb_ref, zbuf, *, T, D, KW):
    t = pl.program_id(1)
    nt = pl.num_programs(1)
    shift = mod_ref[0, 0:1, :]
    scale = mod_ref[0, 1:2, :]
    g = g_ref[...]
    h = _norm_mod(x_ref[0], g, shift, scale).astype(BF16)
    hp = _norm_mod(xp_ref[0], g, shift, scale).astype(BF16)
    hn = _norm_mod(xn_ref[0], g, shift, scale).astype(BF16)

    w0 = w_ref[:, 0:D]
    zbuf[0:SUBLANES, :] = jnp.dot(hp, w0, preferred_element_type=F32) * (t > 0).astype(F32)
    zbuf[SUBLANES:SUBLANES + T, :] = jnp.dot(h, w0, preferred_element_type=F32)
    zbuf[SUBLANES + T:2 * SUBLANES + T, :] = jnp.dot(hn, w0, preferred_element_type=F32) * (t < nt - 1).astype(F32)
    uc = jnp.broadcast_to(cb_ref[...], (T, D))
    for k in range(KW):
        uc = uc + cw_ref[k:k + 1, :] * zbuf[SUBLANES - RNN_PAD_L + k:SUBLANES - RNN_PAD_L + k + T, :]
    uc_ref[0] = uc.astype(BF16)

    gg_ref[0] = _gelu_tanh(jnp.dot(h, w_ref[:, D:2 * D], preferred_element_type=F32)).astype(BF16)
    vg = jnp.dot(h, w_ref[:, 2 * D:3 * D], preferred_element_type=F32)
    gate = _sigmoid(jnp.dot(h, w_ref[:, 3 * D:4 * D], preferred_element_type=F32))
    v_ref[0] = (vg * gate).astype(BF16)
    ga_ref[0] = _sigmoid(jnp.dot(h, w_ref[:, 4 * D:5 * D], preferred_element_type=F32)).astype(BF16)
    gb_ref[0] = _sigmoid(jnp.dot(h, w_ref[:, 5 * D:6 * D], preferred_element_type=F32)).astype(BF16)


def _inproj_call(x, mod, g1, w_in, cw, cb, T):
    B, L, D = x.shape
    KW = cw.shape[0]
    nt = L // T
    per_batch = mod.shape[0] > 1
    tb = T // SUBLANES
    last = L // SUBLANES - 1
    tile = pl.BlockSpec((1, T, D), lambda b, t: (b, t, 0))
    out_sds = jax.ShapeDtypeStruct((B, L, D), BF16)
    return pl.pallas_call(
        functools.partial(_inproj_kernel, T=T, D=D, KW=KW),
        grid=(B, nt),
        in_specs=[
            tile,
            pl.BlockSpec((1, SUBLANES, D), lambda b, t: (b, jnp.maximum(t * tb - 1, 0), 0)),
            pl.BlockSpec((1, SUBLANES, D), lambda b, t: (b, jnp.minimum((t + 1) * tb, last), 0)),
            pl.BlockSpec((1, 6, D), (lambda b, t: (b, 0, 0)) if per_batch else (lambda b, t: (0, 0, 0))),
            _const_spec((1, D)),
            _const_spec(w_in.shape),
            _const_spec((KW, D)),
            _const_spec((1, D)),
        ],
        out_specs=[tile] * 5,
        out_shape=[out_sds] * 5,
        scratch_shapes=[pltpu.VMEM((T + 2 * SUBLANES, D), F32)],
        compiler_params=_params(2),
        name="in_projection",
    )(x, x, x, mod, g1, w_in, cw, cb)


def _convb_kernel(v_ref, vp_ref, vn_ref, cw_ref, cb_ref, lg_ref, lb_ref, wb_ref, gmb_ref, o_ref,
                  vbuf, cbuf, ybuf, *, T, D, HB, S, RC, KW):
    t = pl.program_id(1)
    nt = pl.num_programs(1)
    vbuf[0:HB, :] = vp_ref[0].astype(F32) * (t > 0).astype(F32)
    vbuf[HB:HB + T, :] = v_ref[0].astype(F32)
    vbuf[HB + T:2 * HB + T, :] = vn_ref[0].astype(F32) * (t < nt - 1).astype(F32)
    pad = KW // 2

    def norm_act(r0, rows):
        x = cbuf[pl.ds(r0, rows), :]
        mu = jnp.mean(x, axis=-1, keepdims=True)
        xc = x - mu
        var = jnp.mean(xc * xc, axis=-1, keepdims=True)
        y = xc * lax.rsqrt(var + EPS) * lg_ref[...] + lb_ref[...]
        ybuf[pl.ds(r0, rows), :] = (y * _sigmoid(y)).astype(BF16)

    if S % SUBLANES == 0:
        CR = 2 * S

        def conv_rows(i, carry):
            r0 = pl.multiple_of(i * CR, CR)
            for lc in range(D // LANES):
                ln = slice(lc * LANES, (lc + 1) * LANES)
                acc0 = jnp.broadcast_to(cb_ref[:, ln], (S, LANES))
                acc1 = acc0
                for p in range(KW + 1):
                    piece = vbuf[pl.ds(r0 + (HB + (p - pad) * S), S), ln]
                    if p < KW:
                        acc0 = acc0 + cw_ref[p:p + 1, ln] * piece
                    if p >= 1:
                        acc1 = acc1 + cw_ref[p - 1:p, ln] * piece
                cbuf[pl.ds(r0, S), ln] = acc0
                cbuf[pl.ds(r0 + S, S), ln] = acc1
            norm_act(r0, CR)
            return carry

        lax.fori_loop(0, T // CR, conv_rows, 0)
    else:
        for i in range(T // RC):
            acc = jnp.broadcast_to(cb_ref[...], (RC, D))
            for k in range(KW):
                acc = acc + cw_ref[k:k + 1, :] * vbuf[i * RC + HB + (k - pad) * S:i * RC + HB + (k - pad) * S + RC, :]
            cbuf[i * RC:(i + 1) * RC, :] = acc
            norm_act(i * RC, RC)
    yb = jnp.dot(ybuf[...], wb_ref[...], preferred_element_type=F32)
    o_ref[0] = (yb * gmb_ref[0].astype(F32)).astype(BF16)


def _convb_call(v, gmb, cw, cb, lg, lb, wb, T, HB, S):
    B, L, D = v.shape
    KW = cw.shape[0]
    assert (KW // 2) * S <= HB and T % HB == 0 and L % T == 0 and T % (2 * S) == 0
    nt = L // T
    r = T // HB
    last = L // HB - 1
    tile = pl.BlockSpec((1, T, D), lambda b, t: (b, t, 0))
    return pl.pallas_call(
        functools.partial(_convb_kernel, T=T, D=D, HB=HB, S=S, RC=BF16_ROWS, KW=KW),
        grid=(B, nt),
        in_specs=[
            tile,
            pl.BlockSpec((1, HB, D), lambda b, t: (b, jnp.maximum(t * r - 1, 0), 0)),
            pl.BlockSpec((1, HB, D), lambda b, t: (b, jnp.minimum((t + 1) * r, last), 0)),
            _const_spec(cw.shape),
            _const_spec((1, D)),
            _const_spec((1, D)),
            _const_spec((1, D)),
            _const_spec((D, D)),
            tile,
        ],
        out_specs=tile,
        out_shape=jax.ShapeDtypeStruct((B, L, D), BF16),
        scratch_shapes=[pltpu.VMEM((T + 2 * HB, D), F32), pltpu.VMEM((T, D), F32), pltpu.VMEM((T, D), BF16)],
        compiler_params=_params(2),
        name="conv_branch",
    )(v, v, v, cw, cb, lg, lb, wb, gmb)


def _rnn_kernel(*refs, T, D, reverse, final, write_h):
    uc_ref, wg_ref, br_ref, bi_ref, lam_ref, h0_ref = refs[:6]
    pos = 6
    if final:
        hf_ref, gg_ref, ga_ref, yb_ref, x_ref, mod_ref, wa_ref, wo_ref = refs[pos:pos + 8]
        pos += 8
    outs = []
    if write_h:
        outs.append(refs[pos]); pos += 1
    st_ref = refs[pos]; pos += 1
    if final:
        xo_ref = refs[pos]; pos += 1
    a_buf, b_buf, hcar = refs[pos:pos + 3]

    t = pl.program_id(1)

    @pl.when(t == 0)
    def _():
        hcar[...] = jnp.broadcast_to(h0_ref[0], (SUBLANES, D))

    _lru_coefficients(lambda sl: uc_ref[0, :, sl], wg_ref, br_ref, bi_ref, lam_ref, a_buf, b_buf, D)

    n_blk = T // SUBLANES

    def body(i, h):
        blk = (n_blk - 1 - i) if reverse else i
        rows = pl.ds(pl.multiple_of(blk * SUBLANES, SUBLANES), SUBLANES)
        hh, h_next = _scan_rows(a_buf[rows, :], b_buf[rows, :], h, reverse)
        b_buf[rows, :] = hh
        return h_next

    SUB = min(SCAN_SUB, T)

    def project(n_done):
        rows_done = n_done * SUBLANES
        if rows_done == 0 or rows_done % SUB:
            return
        rows = slice(T - rows_done, T - rows_done + SUB) if reverse else slice(rows_done - SUB, rows_done)
        y = ((hf_ref[0, rows, :].astype(F32) + b_buf[rows, :]) * gg_ref[0, rows, :].astype(F32)).astype(BF16)
        ya = jnp.dot(y, wa_ref[...], preferred_element_type=F32)
        m = (ga_ref[0, rows, :].astype(F32) * ya + yb_ref[0, rows, :].astype(F32)).astype(BF16)
        out = jnp.dot(m, wo_ref[...], preferred_element_type=F32)
        xo_ref[0, rows, :] = x_ref[0, rows, :] + mod_ref[0, 2:3, :] * out

    if final:
        h_end = _scan_interleaved(a_buf, b_buf, hcar[...], T, reverse, project)
    else:
        h_end = lax.fori_loop(0, n_blk, body, hcar[...], unroll=4)
    hcar[...] = h_end
    st_ref[0] = h_end[0:1, :]
    if write_h:
        outs[0][0] = b_buf[...].astype(BF16)


def _rnn_call(uc, wg, br, bi, lam, h0, T, *, reverse, final_args=None, write_h=True):
    B, L, D = uc.shape
    nt = L // T
    final = final_args is not None
    tmap = (lambda b, t: (b, nt - 1 - t, 0)) if reverse else (lambda b, t: (b, t, 0))
    tile = pl.BlockSpec((1, T, D), tmap)
    per_batch_h0 = h0.shape[0] > 1
    in_specs = [
        tile,
        _const_spec(wg.shape),
        _const_spec((1, D)),
        _const_spec((1, D)),
        _const_spec((1, D)),
        pl.BlockSpec((1, 1, D), (lambda b, t: (b, 0, 0)) if per_batch_h0 else (lambda b, t: (0, 0, 0))),
    ]
    args = [uc, wg, br, bi, lam, h0]
    out_specs, out_shape = [], []
    if write_h:
        out_specs.append(tile)
        out_shape.append(jax.ShapeDtypeStruct((B, L, D), BF16))
    out_specs.append(pl.BlockSpec((1, 1, D), lambda b, t: (b, 0, 0)))
    out_shape.append(jax.ShapeDtypeStruct((B, 1, D), F32))
    if final:
        hf, gg, ga, yb, x, mod, wa, wo = final_args
        per_batch = mod.shape[0] > 1
        in_specs += [tile, tile, tile, tile, tile,
                     pl.BlockSpec((1, 6, D), (lambda b, t: (b, 0, 0)) if per_batch else (lambda b, t: (0, 0, 0))),
                     _const_spec((D, D)), _const_spec((D, D))]
        args += [hf, gg, ga, yb, x, mod, wa, wo]
        out_specs.append(tile)
        out_shape.append(jax.ShapeDtypeStruct((B, L, D), F32))
    return pl.pallas_call(
        functools.partial(_rnn_kernel, T=T, D=D, reverse=reverse, final=final, write_h=write_h),
        grid=(B, nt),
        in_specs=in_specs,
        out_specs=out_specs,
        out_shape=out_shape,
        scratch_shapes=[pltpu.VMEM((T, D), F32), pltpu.VMEM((T, D), F32), pltpu.VMEM((SUBLANES, D), F32)],
        compiler_params=_params(2),
        name="rnn_bwd_out" if final else ("rnn_bwd" if reverse else "rnn_fwd"),
    )(*args)


def _swiglu_rows(hb, w1_ref, w3_ref, w2_ref, FC):
    F = w1_ref.shape[-1]
    acc = None
    for c in range(F // FC):
        sl = slice(c * FC, (c + 1) * FC)
        a = jnp.dot(hb, w1_ref[:, sl], preferred_element_type=F32)
        b = jnp.dot(hb, w3_ref[:, sl], preferred_element_type=F32)
        act = (a * _sigmoid(a) * b).astype(BF16)
        part = jnp.dot(act, w2_ref[sl, :], preferred_element_type=F32)
        acc = part if acc is None else acc + part
    return acc


def _ffn_kernel(x_ref, mod_ref, g_ref, w1_ref, w3_ref, w2_ref, *rest, FC, final_norm):
    if final_norm:
        fg_ref, o_ref = rest
    else:
        (o_ref,) = rest
    x = x_ref[0]
    hb = _norm_mod(x, g_ref[...], mod_ref[0, 3:4, :], mod_ref[0, 4:5, :]).astype(BF16)
    y = x + mod_ref[0, 5:6, :] * _swiglu_rows(hb, w1_ref, w3_ref, w2_ref, FC)
    if final_norm:
        y = y * lax.rsqrt(jnp.mean(y * y, axis=-1, keepdims=True) + EPS) * fg_ref[...]
    o_ref[0] = y


def _ffn_call(x, mod, g2, w1, w3, w2, T, final_g=None):
    B, L, D = x.shape
    F = w1.shape[-1]
    per_batch = mod.shape[0] > 1
    tile = pl.BlockSpec((1, T, D), lambda b, t: (b, t, 0))
    in_specs = [tile,
                pl.BlockSpec((1, 6, D), (lambda b, t: (b, 0, 0)) if per_batch else (lambda b, t: (0, 0, 0))),
                _const_spec((1, D)), _const_spec((D, F)), _const_spec((D, F)), _const_spec((F, D))]
    args = [x, mod, g2, w1, w3, w2]
    if final_g is not None:
        in_specs.append(_const_spec((1, D)))
        args.append(final_g)
    return pl.pallas_call(
        functools.partial(_ffn_kernel, FC=_ff_chunk(F), final_norm=final_g is not None),
        grid=(B, L // T),
        in_specs=in_specs,
        out_specs=tile,
        out_shape=jax.ShapeDtypeStruct((B, L, D), F32),
        compiler_params=_params(2),
        name="dense_swiglu",
    )(*args)


def _ff_chunk(F):
    for fc in (512, 256, 128):
        if F % fc == 0:
            return fc
    return F


def _route_kernel(x_ref, mod_ref, g_ref, rw_ref, hb_ref, pos_ref, post_ref, comb_ref, cnt_ref, *, T, E):
    h = _norm_mod(x_ref[...], g_ref[...], mod_ref[0, 3:4, :], mod_ref[0, 4:5, :])
    hb_ref[...] = h.astype(BF16)
    logits = jnp.dot(h, rw_ref[...], precision=lax.Precision.HIGHEST, preferred_element_type=F32)
    lane = lax.broadcasted_iota(jnp.int32, logits.shape, 1).astype(F32)
    neg = jnp.float32(-jnp.inf)
    lg = jnp.where(lane < E, logits, neg)
    m1 = jnp.max(lg, axis=-1, keepdims=True)
    i1 = jnp.min(jnp.where(lg == m1, lane, float(LANES)), axis=-1, keepdims=True)
    lg2 = jnp.where(lane == i1, neg, lg)
    m2 = jnp.max(lg2, axis=-1, keepdims=True)
    i2 = jnp.min(jnp.where(lg2 == m2, lane, float(LANES)), axis=-1, keepdims=True)
    e2 = jnp.exp(m2 - m1)
    den = 1.0 + e2
    sel1 = lane == i1
    sel2 = lane == i2
    comb_ref[...] = jnp.where(sel1, 1.0 / den, jnp.where(sel2, e2 / den, 0.0))
    sel = jnp.where(sel1, 1.0, jnp.where(sel2, 1.0, 0.0))
    row = lax.broadcasted_iota(jnp.int32, (T, T), 0)
    col = lax.broadcasted_iota(jnp.int32, (T, T), 1)
    tri = jnp.where(col < row, 1.0, 0.0).astype(BF16)
    rank = jnp.dot(tri, sel.astype(BF16), preferred_element_type=F32)
    pos = jnp.where(sel > 0.0, rank, -1.0)
    pos_ref[...] = pos
    post_ref[0] = jnp.transpose(pos)[0:SUBLANES, :]
    cnt_ref[0] = jnp.sum(sel, axis=0, keepdims=True)


def _route_call(x2, mod, g2, rw, T, tiles_per_batch):
    N, D = x2.shape
    E = rw.shape[1]
    assert E <= SUBLANES
    nT = N // T
    rw_pad = jnp.zeros((D, LANES), F32).at[:, :E].set(rw)
    per_batch = mod.shape[0] > 1
    tile = pl.BlockSpec((T, D), lambda j: (j, 0))
    col = pl.BlockSpec((T, LANES), lambda j: (j, 0))
    return pl.pallas_call(
        functools.partial(_route_kernel, T=T, E=E),
        grid=(nT,),
        in_specs=[tile,
                  pl.BlockSpec((1, 6, D), (lambda j: (j // tiles_per_batch, 0, 0)) if per_batch else (lambda j: (0, 0, 0))),
                  _const_spec((1, D)), _const_spec((D, LANES))],
        out_specs=[tile, col, pl.BlockSpec((1, SUBLANES, T), lambda j: (j, 0, 0)), col,
                   pl.BlockSpec((1, 1, LANES), lambda j: (j, 0, 0))],
        out_shape=[jax.ShapeDtypeStruct((N, D), BF16), jax.ShapeDtypeStruct((N, LANES), F32),
                   jax.ShapeDtypeStruct((nT, SUBLANES, T), F32), jax.ShapeDtypeStruct((N, LANES), F32),
                   jax.ShapeDtypeStruct((nT, 1, LANES), F32)],
        compiler_params=_params(1),
        name="moe_route",
    )(x2, mod, g2, rw_pad)


def _compact_kernel(cnt_ref, start_ref, hb_ref, post_ref, hs_in_ref, hs_ref, buf, sem, busy, *, T, E, CH):
    del hs_in_ref
    j = pl.program_id(0)
    nj = pl.num_programs(0)

    @pl.when(j == 0)
    def _():
        for e in range(E):
            busy[e] = 0

    def copy(e, row):
        return pltpu.make_async_copy(buf.at[e], hs_ref.at[pl.ds(row, CH), :], sem.at[e])

    hbt = hb_ref[...]
    riota = lax.broadcasted_iota(jnp.int32, (CH, T), 0).astype(F32)
    for e in range(E):
        n = cnt_ref[j * E + e]
        s0 = start_ref[j * E + e]
        pe = post_ref[0, e:e + 1, :]

        def chunk(c, carry, e=e, pe=pe, s0=s0):
            base = c * CH

            @pl.when(busy[e] == 1)
            def _():
                copy(e, 0).wait()

            onehot = jnp.where(pe == riota + base.astype(F32), 1.0, 0.0).astype(BF16)
            buf[e] = jnp.dot(onehot, hbt, preferred_element_type=F32).astype(BF16)
            copy(e, pl.multiple_of(s0 + base, BF16_ROWS)).start()
            busy[e] = 1
            return carry

        lax.fori_loop(0, (n + CH - 1) // CH, chunk, 0)

    @pl.when(j == nj - 1)
    def _():
        for e in range(E):
            @pl.when(busy[e] == 1)
            def _():
                copy(e, 0).wait()


def _compact_call(cnt, start, hb, post, hs0, T, E, CH):
    N, D = hb.shape
    nT = N // T
    grid_spec = pltpu.PrefetchScalarGridSpec(
        num_scalar_prefetch=2,
        grid=(nT,),
        in_specs=[pl.BlockSpec((T, D), lambda j, *_: (j, 0)),
                  pl.BlockSpec((1, SUBLANES, T), lambda j, *_: (j, 0, 0)),
                  pl.BlockSpec(memory_space=pl.ANY)],
        out_specs=pl.BlockSpec(memory_space=pl.ANY),
        scratch_shapes=[pltpu.VMEM((E, CH, D), BF16), pltpu.SemaphoreType.DMA((E,)), pltpu.SMEM((E,), jnp.int32)],
    )
    return pl.pallas_call(
        functools.partial(_compact_kernel, T=T, E=E, CH=CH),
        grid_spec=grid_spec,
        out_shape=jax.ShapeDtypeStruct(hs0.shape, BF16),
        input_output_aliases={4: 0},
        compiler_params=_params(1),
        name="moe_compact",
    )(cnt, start, hb, post, hs0)


def _gffn_kernel(te_ref, rows_ref, hs_ref, w1_ref, w3_ref, w2_ref, ys_ref, *, FC, TM):
    del te_ref
    n = rows_ref[pl.program_id(0)]
    half = TM // 2
    w = (w1_ref.at[0], w3_ref.at[0], w2_ref.at[0])

    @pl.when(n > half)
    def _():
        ys_ref[...] = _swiglu_rows(hs_ref[...], *w, FC).astype(BF16)

    @pl.when((n > 0) & (n <= half))
    def _():
        ys_ref[0:half, :] = _swiglu_rows(hs_ref[0:half, :], *w, FC).astype(BF16)
        ys_ref[half:TM, :] = jnp.zeros((TM - half, ys_ref.shape[1]), BF16)

    @pl.when(n == 0)
    def _():
        ys_ref[...] = jnp.zeros(ys_ref.shape, BF16)


def _gffn_call(tile_expert, tile_rows, hs, w1, w3, w2, TM):
    R, D = hs.shape
    E, _, F = w1.shape
    grid_spec = pltpu.PrefetchScalarGridSpec(
        num_scalar_prefetch=2,
        grid=(R // TM,),
        in_specs=[pl.BlockSpec((TM, D), lambda i, te, va: (i, 0)),
                  pl.BlockSpec((1, D, F), lambda i, te, va: (te[i], 0, 0)),
                  pl.BlockSpec((1, D, F), lambda i, te, va: (te[i], 0, 0)),
                  pl.BlockSpec((1, F, D), lambda i, te, va: (te[i], 0, 0))],
        out_specs=pl.BlockSpec((TM, D), lambda i, te, va: (i, 0)),
    )
    return pl.pallas_call(
        functools.partial(_gffn_kernel, FC=_ff_chunk(F), TM=TM),
        grid_spec=grid_spec,
        out_shape=jax.ShapeDtypeStruct((R, D), BF16),
        compiler_params=_params(1),
        name="moe_grouped_swiglu",
    )(tile_expert, tile_rows, hs, w1, w3, w2)


def _combine_kernel(cnt_ref, start_ref, x_ref, mod_ref, pos_ref, comb_ref, ys_ref, *rest, T, E, CH, final_norm):
    if final_norm:
        fg_ref, o_ref, ybuf, xbuf, acc, sem, xsem = rest
    else:
        o_ref, ybuf, xbuf, acc, sem, xsem = rest
    j = pl.program_id(0)
    nj = pl.num_programs(0)
    slot = j % 2

    def first_chunk(jj, sl, e):
        row = pl.multiple_of(start_ref[jj * E + e], BF16_ROWS)
        return pltpu.make_async_copy(ys_ref.at[pl.ds(row, CH), :], ybuf.at[sl, e], sem.at[sl, e])

    @pl.when(j == 0)
    def _():
        for e in range(E):
            first_chunk(0, 0, e).start()

    @pl.when(j + 1 < nj)
    def _():
        for e in range(E):
            first_chunk(j + 1, 1 - slot, e).start()

    liota = lax.broadcasted_iota(jnp.int32, (T, CH), 1).astype(F32)
    for e in range(E):
        n = cnt_ref[j * E + e]
        s0 = start_ref[j * E + e]
        pe = pos_ref[:, e:e + 1]
        we = comb_ref[:, e:e + 1]
        first_chunk(j, slot, e).wait()
        onehot = jnp.where(pe == liota, 1.0, 0.0).astype(BF16)
        part = we * jnp.dot(onehot, ybuf[slot, e], preferred_element_type=F32)
        if e == 0:
            acc[...] = part
        else:
            acc[...] += part

        def extra(c, carry, pe=pe, we=we, s0=s0):
            base = c * CH
            cp = pltpu.make_async_copy(ys_ref.at[pl.ds(pl.multiple_of(s0 + base, BF16_ROWS), CH), :], xbuf, xsem)
            cp.start()
            cp.wait()
            onehot_c = jnp.where(pe == liota + base.astype(F32), 1.0, 0.0).astype(BF16)
            acc[...] += we * jnp.dot(onehot_c, xbuf[...], preferred_element_type=F32)
            return carry

        lax.fori_loop(1, (n + CH - 1) // CH, extra, 0)
    y = x_ref[...] + mod_ref[0, 5:6, :] * acc[...]
    if final_norm:
        y = y * lax.rsqrt(jnp.mean(y * y, axis=-1, keepdims=True) + EPS) * fg_ref[...]
    o_ref[...] = y


def _combine_call(cnt, start, x2, mod, pos, comb, ys, T, E, CH, tiles_per_batch, final_g=None):
    N, D = x2.shape
    per_batch = mod.shape[0] > 1
    tile = pl.BlockSpec((T, D), lambda j, *_: (j, 0))
    col = pl.BlockSpec((T, LANES), lambda j, *_: (j, 0))
    in_specs = [tile,
                pl.BlockSpec((1, 6, D), (lambda j, *_: (j // tiles_per_batch, 0, 0)) if per_batch
                             else (lambda j, *_: (0, 0, 0))),
                col, col, pl.BlockSpec(memory_space=pl.ANY)]
    args = [cnt, start, x2, mod, pos, comb, ys]
    if final_g is not None:
        in_specs.append(pl.BlockSpec((1, D), lambda j, *_: (0, 0)))
        args.append(final_g)
    grid_spec = pltpu.PrefetchScalarGridSpec(
        num_scalar_prefetch=2,
        grid=(N // T,),
        in_specs=in_specs,
        out_specs=tile,
        scratch_shapes=[pltpu.VMEM((2, E, CH, D), BF16), pltpu.VMEM((CH, D), BF16), pltpu.VMEM((T, D), F32),
                        pltpu.SemaphoreType.DMA((2, E)), pltpu.SemaphoreType.DMA(())],
    )
    return pl.pallas_call(
        functools.partial(_combine_kernel, T=T, E=E, CH=CH, final_norm=final_g is not None),
        grid_spec=grid_spec,
        out_shape=jax.ShapeDtypeStruct((N, D), F32),
        compiler_params=_params(1),
        name="moe_combine",
    )(*args)


def _moe(x, mod, g2, rw, w1, w3, w2, final_g=None):
    B, L, D = x.shape
    N = B * L
    E = rw.shape[1]
    T = min(512, L)
    CH = min(256, T)
    TM = 512
    nT = N // T
    x2 = x.reshape(N, D)
    hb, pos, post, comb, cnt_f = _route_call(x2, mod, g2, rw, T, L // T)

    cnt = cnt_f[:, 0, :E].astype(jnp.int32)
    npad = (cnt + BF16_ROWS - 1) // BF16_ROWS * BF16_ROWS
    glen = (npad.sum(0) + CH + TM - 1) // TM * TM
    gend = jnp.cumsum(glen)
    start = (gend - glen)[None, :] + jnp.cumsum(npad, axis=0) - npad
    rows_bound = 2 * N + nT * E * BF16_ROWS + E * (CH + TM)
    R = (rows_bound + TM - 1) // TM * TM
    tile_row = jnp.arange(R // TM, dtype=jnp.int32) * TM
    tile_expert = jnp.minimum(jnp.sum(tile_row[:, None] >= gend[None, :], axis=1), E - 1).astype(jnp.int32)
    data_end = (gend - glen + npad.sum(0))[tile_expert]
    tile_rows = jnp.clip(data_end - tile_row, 0, TM).astype(jnp.int32)
    cnt1, start1 = cnt.reshape(-1), start.reshape(-1).astype(jnp.int32)

    hs = _compact_call(cnt1, start1, hb, post, jnp.zeros((R, D), BF16), T, E, CH)
    ys = _gffn_call(tile_expert, tile_rows, hs, w1, w3, w2, TM)
    out = _combine_call(cnt1, start1, x2, mod, pos, comb, ys, T, E, CH, L // T, final_g)
    return out.reshape(B, L, D)


def _pack_gates(wr, wi):
    H, hd, _ = wr.shape
    per = GATE_GROUP // hd
    G = H // per

    def blockdiag(w):
        w = w.reshape(G, per, hd, hd)
        eye = jnp.eye(per, dtype=w.dtype)
        return jnp.einsum('gpij,pq->gpiqj', w, eye).reshape(G, GATE_GROUP, GATE_GROUP)

    return jnp.concatenate([blockdiag(wr), blockdiag(wi)], axis=-1).astype(BF16)


def _mixer(x, mod, p, stream_tiles, conv_stride, h0_f, h0_b, full):
    T_in, T_rnn, T_cv, HB = stream_tiles
    uc, gg, v, ga, gb = _inproj_call(x, mod, p['g1'], p['w_in'], p['rnn_cw'], p['rnn_cb'], T_in)
    rnn_f = (uc, p['wg'][0], p['br'][0], p['bi'][0], p['lam'][0], h0_f, T_rnn)
    rnn_b = (uc, p['wg'][1], p['br'][1], p['bi'][1], p['lam'][1], h0_b, T_rnn)
    if not full:
        (s_f,) = _rnn_call(*rnn_f, reverse=False, write_h=False)
        (s_b,) = _rnn_call(*rnn_b, reverse=True, write_h=False)
        return None, s_f, s_b
    ybg = _convb_call(v, gb, p['cw'], p['cb'], p['lg'], p['lb'], p['w_b'], T_cv, HB, conv_stride)
    hf, s_f = _rnn_call(*rnn_f, reverse=False)
    s_b, x_new = _rnn_call(*rnn_b, reverse=True, write_h=False,
                           final_args=(hf, gg, ga, ybg, x, mod, p['w_a'], p['w_out']))
    return x_new, s_f, s_b


def kernel(x, c, ctx, c_ctx, mod_w, mod_b, norm1_g, norm2_g, w_in, rnn_conv_w, rnn_conv_b, lru_wr, lru_br, lru_wi, lru_bi, lru_lam, conv_w, conv_b, conv_ln_g, conv_ln_b, w_branch_a, w_branch_b, w_out, ffn_w1, ffn_w3, ffn_w2, moe_router, moe_w1, moe_w3, moe_w2, final_g):
    B, L, D = x.shape
    Lc = ctx.shape[1]
    depth = mod_w.shape[0]
    assert B + 1 <= SUBLANES and D % GATE_GROUP == 0 and GATE_GROUP % lru_wr.shape[-1] == 0

    rows = jnp.zeros((SUBLANES, D), F32).at[:B].set(c).at[B].set(c_ctx)
    mod_all = _mod_call(rows, mod_w, mod_b)

    lat_tiles = (min(512, L), min(512, L), min(1024, L), min(1024, L))
    ctx_tiles = (Lc, Lc, Lc, BF16_ROWS)
    T_ffn = min(512, L)
    zero_state = jnp.zeros((1, 1, D), F32)
    fg = final_g.reshape(1, D)

    for i in range(depth):
        last = i == depth - 1
        mod_x = mod_all[i, :B].reshape(B, 6, D)
        mod_c = mod_all[i, B].reshape(1, 6, D)
        p = dict(
            g1=norm1_g[i].reshape(1, D), w_in=w_in[i].astype(BF16),
            rnn_cw=rnn_conv_w[i], rnn_cb=rnn_conv_b[i].reshape(1, D),
            wg=[_pack_gates(lru_wr[i, d], lru_wi[i, d]) for d in range(2)],
            br=[lru_br[i, d].reshape(1, D) for d in range(2)], bi=[lru_bi[i, d].reshape(1, D) for d in range(2)],
            lam=[lru_lam[i, d].reshape(1, D) for d in range(2)],
            cw=conv_w[i], cb=conv_b[i].reshape(1, D), lg=conv_ln_g[i].reshape(1, D), lb=conv_ln_b[i].reshape(1, D),
            w_a=w_branch_a[i].astype(BF16), w_b=w_branch_b[i].astype(BF16), w_out=w_out[i].astype(BF16))
        g2 = norm2_g[i].reshape(1, D)
        j = i // 2
        dense = i % 2 == 0
        if dense:
            ffn_w = (ffn_w1[j].astype(BF16), ffn_w3[j].astype(BF16), ffn_w2[j].astype(BF16))
        else:
            ffn_w = (moe_router[j], moe_w1[j].astype(BF16), moe_w3[j].astype(BF16), moe_w2[j].astype(BF16))

        ctx_new, s_f, s_b = _mixer(ctx, mod_c, p, ctx_tiles, 1, zero_state, zero_state, full=not last)
        if not last:
            ctx = (_ffn_call(ctx_new, mod_c, g2, *ffn_w, Lc) if dense
                   else _moe(ctx_new, mod_c, g2, *ffn_w))
        x, _, _ = _mixer(x, mod_x, p, lat_tiles, GRID_W, s_f, s_b, full=True)
        fgi = fg if last else None
        x = (_ffn_call(x, mod_x, g2, *ffn_w, T_ffn, fgi) if dense else _moe(x, mod_x, g2, *ffn_w, fgi))
    return x
```

```python
import functools

import jax
import jax.numpy as jnp
from jax import lax
from jax.experimental import pallas as pl
from jax.experimental.pallas import tpu as pltpu

EPS = 1e-6
LRU_C = 8.0
LOG2_E = 1.4426950408889634
GRID_W = 64
RNN_PAD_L = 2
GATE_GROUP = 256
SUBLANES = 8
LANES = 128
BF16_ROWS = 16
SCAN_SUB = 128
VMEM_LIMIT = 56 * 1024 * 1024

F32 = jnp.float32
BF16 = jnp.bfloat16


def _sigmoid(x):
    return 0.5 * jnp.tanh(0.5 * x) + 0.5


def _gelu_tanh(x):
    return 0.5 * x * (1.0 + jnp.tanh(0.7978845608028654 * (x + 0.044715 * (x * x * x))))


def _norm_mod(x, g, shift, scale):
    ms = jnp.mean(x * x, axis=-1, keepdims=True)
    y = x * lax.rsqrt(ms + EPS) * g
    return y * (1.0 + scale) + shift


def _params(n_axes):
    return pltpu.CompilerParams(dimension_semantics=("arbitrary",) * n_axes, vmem_limit_bytes=VMEM_LIMIT)


def _const_spec(shape):
    nd = len(shape)
    return pl.BlockSpec(shape, lambda *_: (0,) * nd, pipeline_mode=pl.Buffered(1))


def _lru_coefficients(read_u, wg_ref, br_ref, bi_ref, lam_ref, a_buf, b_buf, D):
    z = -lam_ref[...]
    clam = -LRU_C * (jnp.maximum(z, 0.0) + jnp.log1p(jnp.exp(-jnp.abs(z))))
    c2 = (0.5 * LOG2_E) * clam
    GW = GATE_GROUP
    for g in range(D // GW):
        sl = slice(g * GW, (g + 1) * GW)
        ug = read_u(sl)
        pre = jnp.dot(ug, wg_ref[g], preferred_element_type=F32)
        tr = jnp.tanh(pre[:, :GW] + br_ref[:, sl])
        ti = jnp.tanh(pre[:, GW:] + bi_ref[:, sl])
        a = jnp.exp2(c2[:, sl] * tr + c2[:, sl])
        a_buf[:, sl] = a
        q = 1.0 - a * a
        root = jnp.where(q > 0.0, q * lax.rsqrt(q), 0.0)
        uh = 0.5 * ug.astype(F32)
        b_buf[:, sl] = root * (ti * uh + uh)


def _scan_rows(a, b, h, reverse):
    sub = lax.broadcasted_iota(jnp.int32, a.shape, 0)
    for k in (1, 2, 4):
        if reverse:
            ar, br, m = pltpu.roll(a, SUBLANES - k, 0), pltpu.roll(b, SUBLANES - k, 0), sub < SUBLANES - k
        else:
            ar, br, m = pltpu.roll(a, k, 0), pltpu.roll(b, k, 0), sub >= k
        b = jnp.where(m, a * br + b, b)
        a = jnp.where(m, a * ar, a)
    hh = b + a * h
    edge = hh[0:1, :] if reverse else hh[SUBLANES - 1:SUBLANES, :]
    return hh, jnp.broadcast_to(edge, hh.shape)


def _scan_interleaved(a_buf, b_buf, h, T, reverse, mxu_work):
    n_blk = T // SUBLANES
    mxu_work(0)
    for i in range(n_blk):
        r0 = (n_blk - 1 - i if reverse else i) * SUBLANES
        hh, h = _scan_rows(a_buf[r0:r0 + SUBLANES, :], b_buf[r0:r0 + SUBLANES, :], h, reverse)
        b_buf[r0:r0 + SUBLANES, :] = hh
        mxu_work(i + 1)
    return h


def _mod_kernel(rows_ref, w_ref, b_ref, o_ref):
    r = rows_ref[...]
    s = r * _sigmoid(r)
    o_ref[0] = jnp.dot(s, w_ref[0], precision=lax.Precision.HIGHEST, preferred_element_type=F32) + b_ref[0]


def _mod_call(rows, mod_w, mod_b):
    depth, d, n6 = mod_w.shape
    tn = d
    return pl.pallas_call(
        _mod_kernel,
        grid=(depth, n6 // tn),
        in_specs=[
            pl.BlockSpec((SUBLANES, d), lambda l, n: (0, 0)),
            pl.BlockSpec((1, d, tn), lambda l, n: (l, 0, n)),
            pl.BlockSpec((1, 1, tn), lambda l, n: (l, 0, n)),
        ],
        out_specs=pl.BlockSpec((1, SUBLANES, tn), lambda l, n: (l, 0, n)),
        out_shape=jax.ShapeDtypeStruct((depth, SUBLANES, n6), F32),
        compiler_params=_params(2),
        name="adaln_mod",
    )(rows, mod_w, mod_b.reshape(depth, 1, n6))


def _inproj_kernel(x_ref, xp_ref, xn_ref, mod_ref, g_ref, w_ref, cw_ref, cb_ref,
                   uc_ref, gg_ref, v_ref, ga_ref, gb_ref, zbuf, *, T, D, KW):
    t = pl.program_id(1)
    nt = pl.num_programs(1)
    shift = mod_ref[0, 0:1, :]
    scale = mod_ref[0, 1:2, :]
    g = g_ref[...]
    h = _norm_mod(x_ref[0], g, shift, scale).astype(BF16)
    hp = _norm_mod(xp_ref[0], g, shift, scale).astype(BF16)
    hn = _norm_mod(xn_ref[0], g, shift, scale).astype(BF16)

    w0 = w_ref[:, 0:D]
    zbuf[0:SUBLANES, :] = jnp.dot(hp, w0, preferred_element_type=F32) * (t > 0).astype(F32)
    zbuf[SUBLANES:SUBLANES + T, :] = jnp.dot(h, w0, preferred_element_type=F32)
    zbuf[SUBLANES + T:2 * SUBLANES + T, :] = jnp.dot(hn, w0, preferred_element_type=F32) * (t < nt - 1).astype(F32)
    uc = jnp.broadcast_to(cb_ref[...], (T, D))
    for k in range(KW):
        uc = uc + cw_ref[k:k + 1, :] * zbuf[SUBLANES - RNN_PAD_L + k:SUBLANES - RNN_PAD_L + k + T, :]
    uc_ref[0] = uc.astype(BF16)

    gg_ref[0] = _gelu_tanh(jnp.dot(h, w_ref[:, D:2 * D], preferred_element_type=F32)).astype(BF16)
    vh = jnp.dot(h, w_ref[:, 2 * D:3 * D], preferred_element_type=F32)
    tg = jnp.tanh(jnp.dot(h, w_ref[:, 3 * D:4 * D], preferred_element_type=F32))
    v_ref[0] = (vh * tg + vh).astype(BF16)
    ga_ref[0] = (0.5 * jnp.tanh(jnp.dot(h, w_ref[:, 4 * D:5 * D], preferred_element_type=F32)) + 0.5).astype(BF16)
    gb_ref[0] = (0.5 * jnp.tanh(jnp.dot(h, w_ref[:, 5 * D:6 * D], preferred_element_type=F32)) + 0.5).astype(BF16)


def _inproj_call(x, mod, g1, w_in, cw, cb, T):
    B, L, D = x.shape
    KW = cw.shape[0]
    nt = L // T
    per_batch = mod.shape[0] > 1
    tb = T // SUBLANES
    last = L // SUBLANES - 1
    tile = pl.BlockSpec((1, T, D), lambda b, t: (b, t, 0))
    out_sds = jax.ShapeDtypeStruct((B, L, D), BF16)
    return pl.pallas_call(
        functools.partial(_inproj_kernel, T=T, D=D, KW=KW),
        grid=(B, nt),
        in_specs=[
            tile,
            pl.BlockSpec((1, SUBLANES, D), lambda b, t: (b, jnp.maximum(t * tb - 1, 0), 0)),
            pl.BlockSpec((1, SUBLANES, D), lambda b, t: (b, jnp.minimum((t + 1) * tb, last), 0)),
            pl.BlockSpec((1, 6, D), (lambda b, t: (b, 0, 0)) if per_batch else (lambda b, t: (0, 0, 0))),
            _const_spec((1, D)),
            _const_spec(w_in.shape),
            _const_spec((KW, D)),
            _const_spec((1, D)),
        ],
        out_specs=[tile] * 5,
        out_shape=[out_sds] * 5,
        scratch_shapes=[pltpu.VMEM((T + 2 * SUBLANES, D), F32)],
        compiler_params=_params(2),
        name="in_projection",
    )(x, x, x, mod, g1, w_in, cw, cb)


def _convb_kernel(v_ref, vp_ref, vn_ref, cw_ref, cb_ref, lg_ref, lb_ref, wb_ref, gmb_ref, o_ref,
                  vbuf, cbuf, ybuf, *, T, D, HB, S, RC, KW):
    t = pl.program_id(1)
    nt = pl.num_programs(1)
    vbuf[0:HB, :] = vp_ref[0].astype(F32) * (t > 0).astype(F32)
    vbuf[HB:HB + T, :] = v_ref[0].astype(F32)
    vbuf[HB + T:2 * HB + T, :] = vn_ref[0].astype(F32) * (t < nt - 1).astype(F32)
    pad = KW // 2
    lg_half = 0.5 * lg_ref[...]
    lb_half = 0.5 * lb_ref[...]

    def norm_act(r0, rows):
        x = cbuf[pl.ds(r0, rows), :]
        mu = jnp.mean(x, axis=-1, keepdims=True)
        xc = x - mu
        var = jnp.mean(xc * xc, axis=-1, keepdims=True)
        yh = xc * lax.rsqrt(var + EPS) * lg_half + lb_half
        ybuf[pl.ds(r0, rows), :] = (yh * jnp.tanh(yh) + yh).astype(BF16)

    if S % SUBLANES == 0:
        CR = 2 * S

        def conv_rows(i, carry):
            r0 = pl.multiple_of(i * CR, CR)
            for lc in range(D // LANES):
                ln = slice(lc * LANES, (lc + 1) * LANES)
                acc0 = jnp.broadcast_to(cb_ref[:, ln], (S, LANES))
                acc1 = acc0
                for p in range(KW + 1):
                    piece = vbuf[pl.ds(r0 + (HB + (p - pad) * S), S), ln]
                    if p < KW:
                        acc0 = acc0 + cw_ref[p:p + 1, ln] * piece
                    if p >= 1:
                        acc1 = acc1 + cw_ref[p - 1:p, ln] * piece
                cbuf[pl.ds(r0, S), ln] = acc0
                cbuf[pl.ds(r0 + S, S), ln] = acc1
            norm_act(r0, CR)
            return carry

        lax.fori_loop(0, T // CR, conv_rows, 0)
    else:
        for i in range(T // RC):
            acc = jnp.broadcast_to(cb_ref[...], (RC, D))
            for k in range(KW):
                acc = acc + cw_ref[k:k + 1, :] * vbuf[i * RC + HB + (k - pad) * S:i * RC + HB + (k - pad) * S + RC, :]
            cbuf[i * RC:(i + 1) * RC, :] = acc
            norm_act(i * RC, RC)
    yb = jnp.dot(ybuf[...], wb_ref[...], preferred_element_type=F32)
    o_ref[0] = (yb * gmb_ref[0].astype(F32)).astype(BF16)


def _convb_call(v, gmb, cw, cb, lg, lb, wb, T, HB, S):
    B, L, D = v.shape
    KW = cw.shape[0]
    assert (KW // 2) * S <= HB and T % HB == 0 and L % T == 0 and T % (2 * S) == 0
    nt = L // T
    r = T // HB
    last = L // HB - 1
    tile = pl.BlockSpec((1, T, D), lambda b, t: (b, t, 0))
    return pl.pallas_call(
        functools.partial(_convb_kernel, T=T, D=D, HB=HB, S=S, RC=BF16_ROWS, KW=KW),
        grid=(B, nt),
        in_specs=[
            tile,
            pl.BlockSpec((1, HB, D), lambda b, t: (b, jnp.maximum(t * r - 1, 0), 0)),
            pl.BlockSpec((1, HB, D), lambda b, t: (b, jnp.minimum((t + 1) * r, last), 0)),
            _const_spec(cw.shape),
            _const_spec((1, D)),
            _const_spec((1, D)),
            _const_spec((1, D)),
            _const_spec((D, D)),
            tile,
        ],
        out_specs=tile,
        out_shape=jax.ShapeDtypeStruct((B, L, D), BF16),
        scratch_shapes=[pltpu.VMEM((T + 2 * HB, D), F32), pltpu.VMEM((T, D), F32), pltpu.VMEM((T, D), BF16)],
        compiler_params=_params(2),
        name="conv_branch",
    )(v, v, v, cw, cb, lg, lb, wb, gmb)


def _rnn_kernel(*refs, T, D, reverse, final, write_h):
    uc_ref, wg_ref, br_ref, bi_ref, lam_ref, h0_ref = refs[:6]
    pos = 6
    if final:
        hf_ref, gg_ref, ga_ref, yb_ref, x_ref, mod_ref, wa_ref, wo_ref = refs[pos:pos + 8]
        pos += 8
    outs = []
    if write_h:
        outs.append(refs[pos]); pos += 1
    st_ref = refs[pos]; pos += 1
    if final:
        xo_ref = refs[pos]; pos += 1
    a_buf, b_buf, hcar = refs[pos:pos + 3]

    t = pl.program_id(1)

    @pl.when(t == 0)
    def _():
        hcar[...] = jnp.broadcast_to(h0_ref[0], (SUBLANES, D))

    _lru_coefficients(lambda sl: uc_ref[0, :, sl], wg_ref, br_ref, bi_ref, lam_ref, a_buf, b_buf, D)

    n_blk = T // SUBLANES

    def body(i, h):
        blk = (n_blk - 1 - i) if reverse else i
        rows = pl.ds(pl.multiple_of(blk * SUBLANES, SUBLANES), SUBLANES)
        hh, h_next = _scan_rows(a_buf[rows, :], b_buf[rows, :], h, reverse)
        b_buf[rows, :] = hh
        return h_next

    SUB = min(SCAN_SUB, T)

    def project(n_done):
        rows_done = n_done * SUBLANES
        if rows_done == 0 or rows_done % SUB:
            return
        rows = slice(T - rows_done, T - rows_done + SUB) if reverse else slice(rows_done - SUB, rows_done)
        y = ((hf_ref[0, rows, :].astype(F32) + b_buf[rows, :]) * gg_ref[0, rows, :].astype(F32)).astype(BF16)
        ya = jnp.dot(y, wa_ref[...], preferred_element_type=F32)
        m = (ga_ref[0, rows, :].astype(F32) * ya + yb_ref[0, rows, :].astype(F32)).astype(BF16)
        out = jnp.dot(m, wo_ref[...], preferred_element_type=F32)
        xo_ref[0, rows, :] = x_ref[0, rows, :] + mod_ref[0, 2:3, :] * out

    if final:
        h_end = _scan_interleaved(a_buf, b_buf, hcar[...], T, reverse, project)
    else:
        h_end = lax.fori_loop(0, n_blk, body, hcar[...], unroll=4)
    hcar[...] = h_end
    st_ref[0] = h_end[0:1, :]
    if write_h:
        outs[0][0] = b_buf[...].astype(BF16)


def _rnn_call(uc, wg, br, bi, lam, h0, T, *, reverse, final_args=None, write_h=True):
    B, L, D = uc.shape
    nt = L // T
    final = final_args is not None
    tmap = (lambda b, t: (b, nt - 1 - t, 0)) if reverse else (lambda b, t: (b, t, 0))
    tile = pl.BlockSpec((1, T, D), tmap)
    per_batch_h0 = h0.shape[0] > 1
    in_specs = [
        tile,
        _const_spec(wg.shape),
        _const_spec((1, D)),
        _const_spec((1, D)),
        _const_spec((1, D)),
        pl.BlockSpec((1, 1, D), (lambda b, t: (b, 0, 0)) if per_batch_h0 else (lambda b, t: (0, 0, 0))),
    ]
    args = [uc, wg, br, bi, lam, h0]
    out_specs, out_shape = [], []
    if write_h:
        out_specs.append(tile)
        out_shape.append(jax.ShapeDtypeStruct((B, L, D), BF16))
    out_specs.append(pl.BlockSpec((1, 1, D), lambda b, t: (b, 0, 0)))
    out_shape.append(jax.ShapeDtypeStruct((B, 1, D), F32))
    if final:
        hf, gg, ga, yb, x, mod, wa, wo = final_args
        per_batch = mod.shape[0] > 1
        in_specs += [tile, tile, tile, tile, tile,
                     pl.BlockSpec((1, 6, D), (lambda b, t: (b, 0, 0)) if per_batch else (lambda b, t: (0, 0, 0))),
                     _const_spec((D, D)), _const_spec((D, D))]
        args += [hf, gg, ga, yb, x, mod, wa, wo]
        out_specs.append(tile)
        out_shape.append(jax.ShapeDtypeStruct((B, L, D), F32))
    return pl.pallas_call(
        functools.partial(_rnn_kernel, T=T, D=D, reverse=reverse, final=final, write_h=write_h),
        grid=(B, nt),
        in_specs=in_specs,
        out_specs=out_specs,
        out_shape=out_shape,
        scratch_shapes=[pltpu.VMEM((T, D), F32), pltpu.VMEM((T, D), F32), pltpu.VMEM((SUBLANES, D), F32)],
        compiler_params=_params(2),
        name="rnn_bwd_out" if final else ("rnn_bwd" if reverse else "rnn_fwd"),
    )(*args)


def _swiglu_rows(hb, w1_ref, w3_ref, w2_ref, FC):
    F = w1_ref.shape[-1]
    acc = None
    for c in range(F // FC):
        sl = slice(c * FC, (c + 1) * FC)
        a = jnp.dot(hb, w1_ref[:, sl], preferred_element_type=F32)
        b = jnp.dot(hb, w3_ref[:, sl], preferred_element_type=F32)
        act = (a * _sigmoid(a) * b).astype(BF16)
        part = jnp.dot(act, w2_ref[sl, :], preferred_element_type=F32)
        acc = part if acc is None else acc + part
    return acc


def _ffn_kernel(x_ref, mod_ref, g_ref, w1_ref, w3_ref, w2_ref, *rest, FC, final_norm):
    if final_norm:
        fg_ref, o_ref = rest
    else:
        (o_ref,) = rest
    x = x_ref[0]
    hb = _norm_mod(x, g_ref[...], mod_ref[0, 3:4, :], mod_ref[0, 4:5, :]).astype(BF16)
    y = x + mod_ref[0, 5:6, :] * _swiglu_rows(hb, w1_ref, w3_ref, w2_ref, FC)
    if final_norm:
        y = y * lax.rsqrt(jnp.mean(y * y, axis=-1, keepdims=True) + EPS) * fg_ref[...]
    o_ref[0] = y


def _ffn_call(x, mod, g2, w1, w3, w2, T, final_g=None):
    B, L, D = x.shape
    F = w1.shape[-1]
    per_batch = mod.shape[0] > 1
    tile = pl.BlockSpec((1, T, D), lambda b, t: (b, t, 0))
    in_specs = [tile,
                pl.BlockSpec((1, 6, D), (lambda b, t: (b, 0, 0)) if per_batch else (lambda b, t: (0, 0, 0))),
                _const_spec((1, D)), _const_spec((D, F)), _const_spec((D, F)), _const_spec((F, D))]
    args = [x, mod, g2, w1, w3, w2]
    if final_g is not None:
        in_specs.append(_const_spec((1, D)))
        args.append(final_g)
    return pl.pallas_call(
        functools.partial(_ffn_kernel, FC=_ff_chunk(F), final_norm=final_g is not None),
        grid=(B, L // T),
        in_specs=in_specs,
        out_specs=tile,
        out_shape=jax.ShapeDtypeStruct((B, L, D), F32),
        compiler_params=_params(2),
        name="dense_swiglu",
    )(*args)


def _ff_chunk(F):
    for fc in (512, 256, 128):
        if F % fc == 0:
            return fc
    return F


def _route_kernel(x_ref, mod_ref, g_ref, rw_ref, hb_ref, pos_ref, post_ref, comb_ref, cnt_ref, *, T, E):
    h = _norm_mod(x_ref[...], g_ref[...], mod_ref[0, 3:4, :], mod_ref[0, 4:5, :])
    h_hi = h.astype(BF16)
    hb_ref[...] = h_hi
    h_lo = (h - h_hi.astype(F32)).astype(BF16)
    both = jnp.dot(h_hi, rw_ref[...], preferred_element_type=F32)
    logits = (both[:, 0:LANES] + both[:, LANES:2 * LANES]
              + jnp.dot(h_lo, rw_ref[:, 0:LANES], preferred_element_type=F32))
    lane = lax.broadcasted_iota(jnp.int32, logits.shape, 1).astype(F32)
    neg = jnp.float32(-jnp.inf)
    lg = jnp.where(lane < E, logits, neg)
    m1 = jnp.max(lg, axis=-1, keepdims=True)
    i1 = jnp.min(jnp.where(lg == m1, lane, float(LANES)), axis=-1, keepdims=True)
    lg2 = jnp.where(lane == i1, neg, lg)
    m2 = jnp.max(lg2, axis=-1, keepdims=True)
    i2 = jnp.min(jnp.where(lg2 == m2, lane, float(LANES)), axis=-1, keepdims=True)
    e2 = jnp.exp(m2 - m1)
    den = 1.0 + e2
    sel1 = lane == i1
    sel2 = lane == i2
    comb_ref[...] = jnp.where(sel1, 1.0 / den, jnp.where(sel2, e2 / den, 0.0))
    sel = jnp.where(sel1, 1.0, jnp.where(sel2, 1.0, 0.0))
    row = lax.broadcasted_iota(jnp.int32, (T, T), 0)
    col = lax.broadcasted_iota(jnp.int32, (T, T), 1)
    tri = jnp.where(col < row, 1.0, 0.0).astype(BF16)
    rank = jnp.dot(tri, sel.astype(BF16), preferred_element_type=F32)
    pos = jnp.where(sel > 0.0, rank, -1.0)
    pos_ref[...] = pos
    post_ref[0] = jnp.transpose(pos)[0:SUBLANES, :]
    cnt_ref[0] = jnp.sum(sel, axis=0, keepdims=True)


def _route_call(x2, mod, g2, rw, T, tiles_per_batch):
    N, D = x2.shape
    E = rw.shape[1]
    assert E <= SUBLANES
    nT = N // T
    rw_pad = jnp.zeros((D, LANES), F32).at[:, :E].set(rw)
    rw_hi = rw_pad.astype(BF16)
    rw_pad = jnp.concatenate([rw_hi, (rw_pad - rw_hi.astype(F32)).astype(BF16)], axis=1)
    per_batch = mod.shape[0] > 1
    tile = pl.BlockSpec((T, D), lambda j: (j, 0))
    col = pl.BlockSpec((T, LANES), lambda j: (j, 0))
    return pl.pallas_call(
        functools.partial(_route_kernel, T=T, E=E),
        grid=(nT,),
        in_specs=[tile,
                  pl.BlockSpec((1, 6, D), (lambda j: (j // tiles_per_batch, 0, 0)) if per_batch else (lambda j: (0, 0, 0))),
                  _const_spec((1, D)), _const_spec((D, 2 * LANES))],
        out_specs=[tile, col, pl.BlockSpec((1, SUBLANES, T), lambda j: (j, 0, 0)), col,
                   pl.BlockSpec((1, 1, LANES), lambda j: (j, 0, 0))],
        out_shape=[jax.ShapeDtypeStruct((N, D), BF16), jax.ShapeDtypeStruct((N, LANES), F32),
                   jax.ShapeDtypeStruct((nT, SUBLANES, T), F32), jax.ShapeDtypeStruct((N, LANES), F32),
                   jax.ShapeDtypeStruct((nT, 1, LANES), F32)],
        compiler_params=_params(1),
        name="moe_route",
    )(x2, mod, g2, rw_pad)


def _compact_kernel(cnt_ref, start_ref, hb_ref, post_ref, hs_in_ref, hs_ref, buf, sem, busy, *, T, E, CH):
    del hs_in_ref
    j = pl.program_id(0)
    nj = pl.num_programs(0)

    @pl.when(j == 0)
    def _():
        for e in range(E):
            busy[e] = 0

    def copy(e, row):
        return pltpu.make_async_copy(buf.at[e], hs_ref.at[pl.ds(row, CH), :], sem.at[e])

    hbt = hb_ref[...]
    riota = lax.broadcasted_iota(jnp.int32, (CH, T), 0).astype(F32)
    for e in range(E):
        n = cnt_ref[j * E + e]
        s0 = start_ref[j * E + e]
        pe = post_ref[0, e:e + 1, :]

        def chunk(c, carry, e=e, pe=pe, s0=s0):
            base = c * CH

            @pl.when(busy[e] == 1)
            def _():
                copy(e, 0).wait()

            onehot = jnp.where(pe == riota + base.astype(F32), 1.0, 0.0).astype(BF16)
            buf[e] = jnp.dot(onehot, hbt, preferred_element_type=F32).astype(BF16)
            copy(e, pl.multiple_of(s0 + base, BF16_ROWS)).start()
            busy[e] = 1
            return carry

        lax.fori_loop(0, (n + CH - 1) // CH, chunk, 0)

    @pl.when(j == nj - 1)
    def _():
        for e in range(E):
            @pl.when(busy[e] == 1)
            def _():
                copy(e, 0).wait()


def _compact_call(cnt, start, hb, post, hs0, T, E, CH):
    N, D = hb.shape
    nT = N // T
    grid_spec = pltpu.PrefetchScalarGridSpec(
        num_scalar_prefetch=2,
        grid=(nT,),
        in_specs=[pl.BlockSpec((T, D), lambda j, *_: (j, 0)),
                  pl.BlockSpec((1, SUBLANES, T), lambda j, *_: (j, 0, 0)),
                  pl.BlockSpec(memory_space=pl.ANY)],
        out_specs=pl.BlockSpec(memory_space=pl.ANY),
        scratch_shapes=[pltpu.VMEM((E, CH, D), BF16), pltpu.SemaphoreType.DMA((E,)), pltpu.SMEM((E,), jnp.int32)],
    )
    return pl.pallas_call(
        functools.partial(_compact_kernel, T=T, E=E, CH=CH),
        grid_spec=grid_spec,
        out_shape=jax.ShapeDtypeStruct(hs0.shape, BF16),
        input_output_aliases={4: 0},
        compiler_params=_params(1),
        name="moe_compact",
    )(cnt, start, hb, post, hs0)


def _gffn_kernel(te_ref, rows_ref, hs_ref, w1_ref, w3_ref, w2_ref, ys_ref, *, FC, TM):
    del te_ref
    n = rows_ref[pl.program_id(0)]
    half = TM // 2
    w = (w1_ref.at[0], w3_ref.at[0], w2_ref.at[0])

    @pl.when(n > half)
    def _():
        ys_ref[...] = _swiglu_rows(hs_ref[...], *w, FC).astype(BF16)

    @pl.when((n > 0) & (n <= half))
    def _():
        ys_ref[0:half, :] = _swiglu_rows(hs_ref[0:half, :], *w, FC).astype(BF16)
        ys_ref[half:TM, :] = jnp.zeros((TM - half, ys_ref.shape[1]), BF16)

    @pl.when(n == 0)
    def _():
        ys_ref[...] = jnp.zeros(ys_ref.shape, BF16)


def _gffn_call(tile_expert, tile_rows, hs, w1, w3, w2, TM):
    R, D = hs.shape
    E, _, F = w1.shape
    grid_spec = pltpu.PrefetchScalarGridSpec(
        num_scalar_prefetch=2,
        grid=(R // TM,),
        in_specs=[pl.BlockSpec((TM, D), lambda i, te, va: (i, 0)),
                  pl.BlockSpec((1, D, F), lambda i, te, va: (te[i], 0, 0)),
                  pl.BlockSpec((1, D, F), lambda i, te, va: (te[i], 0, 0)),
                  pl.BlockSpec((1, F, D), lambda i, te, va: (te[i], 0, 0))],
        out_specs=pl.BlockSpec((TM, D), lambda i, te, va: (i, 0)),
    )
    return pl.pallas_call(
        functools.partial(_gffn_kernel, FC=_ff_chunk(F), TM=TM),
        grid_spec=grid_spec,
        out_shape=jax.ShapeDtypeStruct((R, D), BF16),
        compiler_params=_params(1),
        name="moe_grouped_swiglu",
    )(tile_expert, tile_rows, hs, w1, w3, w2)


def _combine_kernel(cnt_ref, start_ref, x_ref, mod_ref, pos_ref, comb_ref, ys_ref, *rest, T, E, CH, final_norm):
    if final_norm:
        fg_ref, o_ref, ybuf, xbuf, acc, sem, xsem = rest
    else:
        o_ref, ybuf, xbuf, acc, sem, xsem = rest
    j = pl.program_id(0)
    nj = pl.num_programs(0)
    slot = j % 2

    def first_chunk(jj, sl, e):
        row = pl.multiple_of(start_ref[jj * E + e], BF16_ROWS)
        return pltpu.make_async_copy(ys_ref.at[pl.ds(row, CH), :], ybuf.at[sl, e], sem.at[sl, e])

    @pl.when(j == 0)
    def _():
        for e in range(E):
            first_chunk(0, 0, e).start()

    @pl.when(j + 1 < nj)
    def _():
        for e in range(E):
            first_chunk(j + 1, 1 - slot, e).start()

    liota = lax.broadcasted_iota(jnp.int32, (T, CH), 1).astype(F32)
    for e in range(E):
        n = cnt_ref[j * E + e]
        s0 = start_ref[j * E + e]
        pe = pos_ref[:, e:e + 1]
        we = comb_ref[:, e:e + 1]
        first_chunk(j, slot, e).wait()
        onehot = jnp.where(pe == liota, 1.0, 0.0).astype(BF16)
        part = we * jnp.dot(onehot, ybuf[slot, e], preferred_element_type=F32)
        if e == 0:
            acc[...] = part
        else:
            acc[...] += part

        def extra(c, carry, pe=pe, we=we, s0=s0):
            base = c * CH
            cp = pltpu.make_async_copy(ys_ref.at[pl.ds(pl.multiple_of(s0 + base, BF16_ROWS), CH), :], xbuf, xsem)
            cp.start()
            cp.wait()
            onehot_c = jnp.where(pe == liota + base.astype(F32), 1.0, 0.0).astype(BF16)
            acc[...] += we * jnp.dot(onehot_c, xbuf[...], preferred_element_type=F32)
            return carry

        lax.fori_loop(1, (n + CH - 1) // CH, extra, 0)
    y = x_ref[...] + mod_ref[0, 5:6, :] * acc[...]
    if final_norm:
        y = y * lax.rsqrt(jnp.mean(y * y, axis=-1, keepdims=True) + EPS) * fg_ref[...]
    o_ref[...] = y


def _combine_call(cnt, start, x2, mod, pos, comb, ys, T, E, CH, tiles_per_batch, final_g=None):
    N, D = x2.shape
    per_batch = mod.shape[0] > 1
    tile = pl.BlockSpec((T, D), lambda j, *_: (j, 0))
    col = pl.BlockSpec((T, LANES), lambda j, *_: (j, 0))
    in_specs = [tile,
                pl.BlockSpec((1, 6, D), (lambda j, *_: (j // tiles_per_batch, 0, 0)) if per_batch
                             else (lambda j, *_: (0, 0, 0))),
                col, col, pl.BlockSpec(memory_space=pl.ANY)]
    args = [cnt, start, x2, mod, pos, comb, ys]
    if final_g is not None:
        in_specs.append(pl.BlockSpec((1, D), lambda j, *_: (0, 0)))
        args.append(final_g)
    grid_spec = pltpu.PrefetchScalarGridSpec(
        num_scalar_prefetch=2,
        grid=(N // T,),
        in_specs=in_specs,
        out_specs=tile,
        scratch_shapes=[pltpu.VMEM((2, E, CH, D), BF16), pltpu.VMEM((CH, D), BF16), pltpu.VMEM((T, D), F32),
                        pltpu.SemaphoreType.DMA((2, E)), pltpu.SemaphoreType.DMA(())],
    )
    return pl.pallas_call(
        functools.partial(_combine_kernel, T=T, E=E, CH=CH, final_norm=final_g is not None),
        grid_spec=grid_spec,
        out_shape=jax.ShapeDtypeStruct((N, D), F32),
        compiler_params=_params(1),
        name="moe_combine",
    )(*args)


def _moe(x, mod, g2, rw, w1, w3, w2, final_g=None):
    B, L, D = x.shape
    N = B * L
    E = rw.shape[1]
    T = min(512, L)
    CH = min(256, T)
    TM = 512
    nT = N // T
    x2 = x.reshape(N, D)
    hb, pos, post, comb, cnt_f = _route_call(x2, mod, g2, rw, T, L // T)

    cnt = cnt_f[:, 0, :E].astype(jnp.int32)
    npad = (cnt + BF16_ROWS - 1) // BF16_ROWS * BF16_ROWS
    glen = (npad.sum(0) + CH + TM - 1) // TM * TM
    gend = jnp.cumsum(glen)
    start = (gend - glen)[None, :] + jnp.cumsum(npad, axis=0) - npad
    rows_bound = 2 * N + nT * E * BF16_ROWS + E * (CH + TM)
    R = (rows_bound + TM - 1) // TM * TM
    tile_row = jnp.arange(R // TM, dtype=jnp.int32) * TM
    tile_expert = jnp.minimum(jnp.sum(tile_row[:, None] >= gend[None, :], axis=1), E - 1).astype(jnp.int32)
    data_end = (gend - glen + npad.sum(0))[tile_expert]
    tile_rows = jnp.clip(data_end - tile_row, 0, TM).astype(jnp.int32)
    cnt1, start1 = cnt.reshape(-1), start.reshape(-1).astype(jnp.int32)

    hs = _compact_call(cnt1, start1, hb, post, jnp.zeros((R, D), BF16), T, E, CH)
    ys = _gffn_call(tile_expert, tile_rows, hs, w1, w3, w2, TM)
    out = _combine_call(cnt1, start1, x2, mod, pos, comb, ys, T, E, CH, L // T, final_g)
    return out.reshape(B, L, D)


def _pack_gates(wr, wi):
    H, hd, _ = wr.shape
    per = GATE_GROUP // hd
    G = H // per

    def blockdiag(w):
        w = w.reshape(G, per, hd, hd)
        eye = jnp.eye(per, dtype=w.dtype)
        return jnp.einsum('gpij,pq->gpiqj', w, eye).reshape(G, GATE_GROUP, GATE_GROUP)

    return (0.5 * jnp.concatenate([blockdiag(wr), blockdiag(wi)], axis=-1)).astype(BF16)


def _mixer(x, mod, p, stream_tiles, conv_stride, h0_f, h0_b, full):
    T_in, T_rnn, T_cv, HB = stream_tiles
    uc, gg, v, ga, gb = _inproj_call(x, mod, p['g1'], p['w_in'], p['rnn_cw'], p['rnn_cb'], T_in)
    rnn_f = (uc, p['wg'][0], p['br'][0], p['bi'][0], p['lam'][0], h0_f, T_rnn)
    rnn_b = (uc, p['wg'][1], p['br'][1], p['bi'][1], p['lam'][1], h0_b, T_rnn)
    if not full:
        (s_f,) = _rnn_call(*rnn_f, reverse=False, write_h=False)
        (s_b,) = _rnn_call(*rnn_b, reverse=True, write_h=False)
        return None, s_f, s_b
    ybg = _convb_call(v, gb, p['cw'], p['cb'], p['lg'], p['lb'], p['w_b'], T_cv, HB, conv_stride)
    hf, s_f = _rnn_call(*rnn_f, reverse=False)
    s_b, x_new = _rnn_call(*rnn_b, reverse=True, write_h=False,
                           final_args=(hf, gg, ga, ybg, x, mod, p['w_a'], p['w_out']))
    return x_new, s_f, s_b


def kernel(x, c, ctx, c_ctx, mod_w, mod_b, norm1_g, norm2_g, w_in, rnn_conv_w, rnn_conv_b, lru_wr, lru_br, lru_wi, lru_bi, lru_lam, conv_w, conv_b, conv_ln_g, conv_ln_b, w_branch_a, w_branch_b, w_out, ffn_w1, ffn_w3, ffn_w2, moe_router, moe_w1, moe_w3, moe_w2, final_g):
    B, L, D = x.shape
    Lc = ctx.shape[1]
    depth = mod_w.shape[0]
    assert B + 1 <= SUBLANES and D % GATE_GROUP == 0 and GATE_GROUP % lru_wr.shape[-1] == 0

    rows = jnp.zeros((SUBLANES, D), F32).at[:B].set(c).at[B].set(c_ctx)
    mod_all = _mod_call(rows, mod_w, mod_b)

    lat_tiles = (min(512, L), min(512, L), min(1024, L), min(1024, L))
    ctx_tiles = (Lc, Lc, Lc, BF16_ROWS)
    T_ffn = min(512, L)
    zero_state = jnp.zeros((1, 1, D), F32)
    fg = final_g.reshape(1, D)
    in_col_scale = jnp.repeat(jnp.array([1.0, 1.0, 0.5, 0.5, 0.5, 0.5], F32), D)[None, :]

    for i in range(depth):
        last = i == depth - 1
        mod_x = mod_all[i, :B].reshape(B, 6, D)
        mod_c = mod_all[i, B].reshape(1, 6, D)
        p = dict(
            g1=norm1_g[i].reshape(1, D), w_in=(w_in[i] * in_col_scale).astype(BF16),
            rnn_cw=rnn_conv_w[i], rnn_cb=rnn_conv_b[i].reshape(1, D),
            wg=[_pack_gates(lru_wr[i, d], lru_wi[i, d]) for d in range(2)],
            br=[0.5 * lru_br[i, d].reshape(1, D) for d in range(2)],
            bi=[0.5 * lru_bi[i, d].reshape(1, D) for d in range(2)],
            lam=[lru_lam[i, d].reshape(1, D) for d in range(2)],
            cw=conv_w[i], cb=conv_b[i].reshape(1, D), lg=conv_ln_g[i].reshape(1, D), lb=conv_ln_b[i].reshape(1, D),
            w_a=w_branch_a[i].astype(BF16), w_b=w_branch_b[i].astype(BF16), w_out=w_out[i].astype(BF16))
        g2 = norm2_g[i].reshape(1, D)
        j = i // 2
        dense = i % 2 == 0
        if dense:
            ffn_w = (ffn_w1[j].astype(BF16), ffn_w3[j].astype(BF16), ffn_w2[j].astype(BF16))
        else:
            ffn_w = (moe_router[j], moe_w1[j].astype(BF16), moe_w3[j].astype(BF16), moe_w2[j].astype(BF16))

        ctx_new, s_f, s_b = _mixer(ctx, mod_c, p, ctx_tiles, 1, zero_state, zero_state, full=not last)
        if not last:
            ctx = (_ffn_call(ctx_new, mod_c, g2, *ffn_w, Lc) if dense
                   else _moe(ctx_new, mod_c, g2, *ffn_w))
        x, _, _ = _mixer(x, mod_x, p, lat_tiles, GRID_W, s_f, s_b, full=True)
        fgi = fg if last else None
        x = (_ffn_call(x, mod_x, g2, *ffn_w, T_ffn, fgi) if dense else _moe(x, mod_x, g2, *ffn_w, fgi))
    return x
```

```python
import functools

import jax
import jax.numpy as jnp
from jax import lax
from jax.experimental import pallas as pl
from jax.experimental.pallas import tpu as pltpu

EPS = 1e-6
LRU_C = 8.0
LOG2_E = 1.4426950408889634
GRID_W = 64
RNN_PAD_L = 2
GATE_GROUP = 256
SUBLANES = 8
LANES = 128
BF16_ROWS = 16
SCAN_SUB = 128
VMEM_LIMIT = 56 * 1024 * 1024

F32 = jnp.float32
BF16 = jnp.bfloat16


def _sigmoid(x):
    return 0.5 * jnp.tanh(0.5 * x) + 0.5


def _gelu_tanh(x):
    return 0.5 * x * (1.0 + jnp.tanh(0.7978845608028654 * (x + 0.044715 * (x * x * x))))


def _norm_mod(x, g, shift, scale):
    ms = jnp.mean(x * x, axis=-1, keepdims=True)
    y = x * lax.rsqrt(ms + EPS) * g
    return y * (1.0 + scale) + shift


def _params(n_axes):
    return pltpu.CompilerParams(dimension_semantics=("arbitrary",) * n_axes, vmem_limit_bytes=VMEM_LIMIT)


def _const_spec(shape):
    nd = len(shape)
    return pl.BlockSpec(shape, lambda *_: (0,) * nd, pipeline_mode=pl.Buffered(1))


def _lru_coefficients(read_u, wg_ref, br_ref, bi_ref, lam_ref, a_buf, b_buf, D):
    z = -lam_ref[...]
    clam = -LRU_C * (jnp.maximum(z, 0.0) + jnp.log1p(jnp.exp(-jnp.abs(z))))
    c2 = (0.5 * LOG2_E) * clam
    GW = GATE_GROUP
    for g in range(D // GW):
        sl = slice(g * GW, (g + 1) * GW)
        ug = read_u(sl)
        pre = jnp.dot(ug, wg_ref[g], preferred_element_type=F32)
        tr = jnp.tanh(pre[:, :GW] + br_ref[:, sl])
        ti = jnp.tanh(pre[:, GW:] + bi_ref[:, sl])
        a = jnp.exp2(c2[:, sl] * tr + c2[:, sl])
        a_buf[:, sl] = a
        q = 1.0 - a * a
        root = jnp.where(q > 0.0, q * lax.rsqrt(q), 0.0)
        uh = 0.5 * ug.astype(F32)
        b_buf[:, sl] = root * (ti * uh + uh)


def _scan_rows(a, b, h, reverse):
    sub = lax.broadcasted_iota(jnp.int32, a.shape, 0)
    half = SUBLANES // 2
    for k in (1, 2):
        if reverse:
            ar, br = pltpu.roll(a, SUBLANES - k, 0), pltpu.roll(b, SUBLANES - k, 0)
            m = (sub % half) < half - k
        else:
            ar, br = pltpu.roll(a, k, 0), pltpu.roll(b, k, 0)
            m = (sub % half) >= k
        b = jnp.where(m, a * br + b, b)
        a = jnp.where(m, a * ar, a)
    first = b + a * h
    mid = first[half:half + 1, :] if reverse else first[half - 1:half, :]
    second = b + a * jnp.broadcast_to(mid, a.shape)
    hh = jnp.where((sub >= half) if reverse else (sub < half), first, second)
    edge = hh[0:1, :] if reverse else hh[SUBLANES - 1:SUBLANES, :]
    return hh, jnp.broadcast_to(edge, hh.shape)


def _scan_interleaved(a_buf, b_buf, h, T, reverse, mxu_work):
    n_blk = T // SUBLANES
    mxu_work(0)
    for i in range(n_blk):
        r0 = (n_blk - 1 - i if reverse else i) * SUBLANES
        hh, h = _scan_rows(a_buf[r0:r0 + SUBLANES, :], b_buf[r0:r0 + SUBLANES, :], h, reverse)
        b_buf[r0:r0 + SUBLANES, :] = hh
        mxu_work(i + 1)
    return h


def _mod_kernel(rows_ref, w_ref, b_ref, o_ref):
    r = rows_ref[...]
    s = r * _sigmoid(r)
    o_ref[0] = jnp.dot(s, w_ref[0], precision=lax.Precision.HIGHEST, preferred_element_type=F32) + b_ref[0]


def _mod_call(rows, mod_w, mod_b):
    depth, d, n6 = mod_w.shape
    tn = d
    return pl.pallas_call(
        _mod_kernel,
        grid=(depth, n6 // tn),
        in_specs=[
            pl.BlockSpec((SUBLANES, d), lambda l, n: (0, 0)),
            pl.BlockSpec((1, d, tn), lambda l, n: (l, 0, n)),
            pl.BlockSpec((1, 1, tn), lambda l, n: (l, 0, n)),
        ],
        out_specs=pl.BlockSpec((1, SUBLANES, tn), lambda l, n: (l, 0, n)),
        out_shape=jax.ShapeDtypeStruct((depth, SUBLANES, n6), F32),
        compiler_params=_params(2),
        name="adaln_mod",
    )(rows, mod_w, mod_b.reshape(depth, 1, n6))


def _inproj_kernel(x_ref, xp_ref, xn_ref, mod_ref, g_ref, w_ref, cw_ref, cb_ref,
                   uc_ref, gg_ref, v_ref, ga_ref, gb_ref, zbuf, *, T, D, KW):
    t = pl.program_id(1)
    nt = pl.num_programs(1)
    shift = mod_ref[0, 0:1, :]
    scale = mod_ref[0, 1:2, :]
    g = g_ref[...]
    h = _norm_mod(x_ref[0], g, shift, scale).astype(BF16)
    hp = _norm_mod(xp_ref[0], g, shift, scale).astype(BF16)
    hn = _norm_mod(xn_ref[0], g, shift, scale).astype(BF16)

    w0 = w_ref[:, 0:D]
    zbuf[0:SUBLANES, :] = jnp.dot(hp, w0, preferred_element_type=F32) * (t > 0).astype(F32)
    zbuf[SUBLANES:SUBLANES + T, :] = jnp.dot(h, w0, preferred_element_type=F32)
    zbuf[SUBLANES + T:2 * SUBLANES + T, :] = jnp.dot(hn, w0, preferred_element_type=F32) * (t < nt - 1).astype(F32)
    uc = jnp.broadcast_to(cb_ref[...], (T, D))
    for k in range(KW):
        uc = uc + cw_ref[k:k + 1, :] * zbuf[SUBLANES - RNN_PAD_L + k:SUBLANES - RNN_PAD_L + k + T, :]
    uc_ref[0] = uc.astype(BF16)

    gg_ref[0] = _gelu_tanh(jnp.dot(h, w_ref[:, D:2 * D], preferred_element_type=F32)).astype(BF16)
    vh = jnp.dot(h, w_ref[:, 2 * D:3 * D], preferred_element_type=F32)
    tg = jnp.tanh(jnp.dot(h, w_ref[:, 3 * D:4 * D], preferred_element_type=F32))
    v_ref[0] = (vh * tg + vh).astype(BF16)
    ga_ref[0] = (0.5 * jnp.tanh(jnp.dot(h, w_ref[:, 4 * D:5 * D], preferred_element_type=F32)) + 0.5).astype(BF16)
    gb_ref[0] = (0.5 * jnp.tanh(jnp.dot(h, w_ref[:, 5 * D:6 * D], preferred_element_type=F32)) + 0.5).astype(BF16)


def _inproj_call(x, mod, g1, w_in, cw, cb, T):
    B, L, D = x.shape
    KW = cw.shape[0]
    nt = L // T
    per_batch = mod.shape[0] > 1
    tb = T // SUBLANES
    last = L // SUBLANES - 1
    tile = pl.BlockSpec((1, T, D), lambda b, t: (b, t, 0))
    out_sds = jax.ShapeDtypeStruct((B, L, D), BF16)
    return pl.pallas_call(
        functools.partial(_inproj_kernel, T=T, D=D, KW=KW),
        grid=(B, nt),
        in_specs=[
            tile,
            pl.BlockSpec((1, SUBLANES, D), lambda b, t: (b, jnp.maximum(t * tb - 1, 0), 0)),
            pl.BlockSpec((1, SUBLANES, D), lambda b, t: (b, jnp.minimum((t + 1) * tb, last), 0)),
            pl.BlockSpec((1, 6, D), (lambda b, t: (b, 0, 0)) if per_batch else (lambda b, t: (0, 0, 0))),
            _const_spec((1, D)),
            _const_spec(w_in.shape),
            _const_spec((KW, D)),
            _const_spec((1, D)),
        ],
        out_specs=[tile] * 5,
        out_shape=[out_sds] * 5,
        scratch_shapes=[pltpu.VMEM((T + 2 * SUBLANES, D), F32)],
        compiler_params=_params(2),
        name="in_projection",
    )(x, x, x, mod, g1, w_in, cw, cb)


def _convb_kernel(v_ref, vp_ref, vn_ref, cw_ref, cb_ref, lg_ref, lb_ref, wb_ref, gmb_ref, o_ref,
                  vbuf, cbuf, ybuf, *, T, D, HB, S, RC, KW):
    t = pl.program_id(1)
    nt = pl.num_programs(1)
    ring = S % SUBLANES == 0 and HB == T
    if ring:
        @pl.when(t == 0)
        def _():
            vbuf[0:T, :] = v_ref[0].astype(F32)
            vbuf[2 * T:3 * T, :] = jnp.zeros((T, D), F32)

        next_rows = pl.ds(pl.multiple_of(((t + 1) % 3) * T, T), T)

        @pl.when(t < nt - 1)
        def _():
            vbuf[next_rows, :] = vn_ref[0].astype(F32)

        @pl.when(t == nt - 1)
        def _():
            vbuf[next_rows, :] = jnp.zeros((T, D), F32)

        ring_base = (t % 3 + 3) * T
    else:
        vbuf[0:HB, :] = vp_ref[0].astype(F32) * (t > 0).astype(F32)
        vbuf[HB:HB + T, :] = v_ref[0].astype(F32)
        vbuf[HB + T:2 * HB + T, :] = vn_ref[0].astype(F32) * (t < nt - 1).astype(F32)
    pad = KW // 2
    lg_half = 0.5 * lg_ref[...]
    lb_half = 0.5 * lb_ref[...]

    def norm_act(r0, rows):
        x = cbuf[pl.ds(r0, rows), :]
        mu = jnp.mean(x, axis=-1, keepdims=True)
        xc = x - mu
        var = jnp.mean(xc * xc, axis=-1, keepdims=True)
        yh = xc * lax.rsqrt(var + EPS) * lg_half + lb_half
        ybuf[pl.ds(r0, rows), :] = (yh * jnp.tanh(yh) + yh).astype(BF16)

    if S % SUBLANES == 0:
        CR = 2 * S

        def conv_rows(i, carry):
            r0 = pl.multiple_of(i * CR, CR)
            if ring:
                starts = [pl.multiple_of(lax.rem(ring_base + r0 + (p - pad) * S, 3 * T), S) for p in range(KW + 1)]
            else:
                starts = [r0 + (HB + (p - pad) * S) for p in range(KW + 1)]
            for lc in range(D // LANES):
                ln = slice(lc * LANES, (lc + 1) * LANES)
                acc0 = jnp.broadcast_to(cb_ref[:, ln], (S, LANES))
                acc1 = acc0
                for p in range(KW + 1):
                    piece = vbuf[pl.ds(starts[p], S), ln]
                    if p < KW:
                        acc0 = acc0 + cw_ref[p:p + 1, ln] * piece
                    if p >= 1:
                        acc1 = acc1 + cw_ref[p - 1:p, ln] * piece
                cbuf[pl.ds(r0, S), ln] = acc0
                cbuf[pl.ds(r0 + S, S), ln] = acc1
            norm_act(r0, CR)
            return carry

        lax.fori_loop(0, T // CR, conv_rows, 0)
    else:
        for i in range(T // RC):
            acc = jnp.broadcast_to(cb_ref[...], (RC, D))
            for k in range(KW):
                acc = acc + cw_ref[k:k + 1, :] * vbuf[i * RC + HB + (k - pad) * S:i * RC + HB + (k - pad) * S + RC, :]
            cbuf[i * RC:(i + 1) * RC, :] = acc
            norm_act(i * RC, RC)
    yb = jnp.dot(ybuf[...], wb_ref[...], preferred_element_type=F32)
    o_ref[0] = (yb * gmb_ref[0].astype(F32)).astype(BF16)


def _convb_call(v, gmb, cw, cb, lg, lb, wb, T, HB, S):
    B, L, D = v.shape
    KW = cw.shape[0]
    assert (KW // 2) * S <= HB and T % HB == 0 and L % T == 0 and T % (2 * S) == 0
    nt = L // T
    r = T // HB
    last = L // HB - 1
    tile = pl.BlockSpec((1, T, D), lambda b, t: (b, t, 0))
    return pl.pallas_call(
        functools.partial(_convb_kernel, T=T, D=D, HB=HB, S=S, RC=BF16_ROWS, KW=KW),
        grid=(B, nt),
        in_specs=[
            tile,
            pl.BlockSpec((1, HB, D), lambda b, t: (b, jnp.maximum(t * r - 1, 0), 0)),
            pl.BlockSpec((1, HB, D), lambda b, t: (b, jnp.minimum((t + 1) * r, last), 0)),
            _const_spec(cw.shape),
            _const_spec((1, D)),
            _const_spec((1, D)),
            _const_spec((1, D)),
            _const_spec((D, D)),
            tile,
        ],
        out_specs=tile,
        out_shape=jax.ShapeDtypeStruct((B, L, D), BF16),
        scratch_shapes=[pltpu.VMEM((T + 2 * HB, D), F32), pltpu.VMEM((T, D), F32), pltpu.VMEM((T, D), BF16)],
        compiler_params=_params(2),
        name="conv_branch",
    )(v, v, v, cw, cb, lg, lb, wb, gmb)


def _rnn_kernel(*refs, T, D, reverse, final, write_h):
    uc_ref, wg_ref, br_ref, bi_ref, lam_ref, h0_ref = refs[:6]
    pos = 6
    if final:
        hf_ref, gg_ref, ga_ref, yb_ref, x_ref, mod_ref, wa_ref, wo_ref = refs[pos:pos + 8]
        pos += 8
    outs = []
    if write_h:
        outs.append(refs[pos]); pos += 1
    st_ref = refs[pos]; pos += 1
    if final:
        xo_ref = refs[pos]; pos += 1
    a_buf, b_buf, hcar = refs[pos:pos + 3]

    t = pl.program_id(1)

    @pl.when(t == 0)
    def _():
        hcar[...] = jnp.broadcast_to(h0_ref[0], (SUBLANES, D))

    _lru_coefficients(lambda sl: uc_ref[0, :, sl], wg_ref, br_ref, bi_ref, lam_ref, a_buf, b_buf, D)

    n_blk = T // SUBLANES

    def body(i, h):
        blk = (n_blk - 1 - i) if reverse else i
        rows = pl.ds(pl.multiple_of(blk * SUBLANES, SUBLANES), SUBLANES)
        hh, h_next = _scan_rows(a_buf[rows, :], b_buf[rows, :], h, reverse)
        b_buf[rows, :] = hh
        return h_next

    SUB = min(SCAN_SUB, T)

    def project(n_done):
        rows_done = n_done * SUBLANES
        if rows_done == 0 or rows_done % SUB:
            return
        rows = slice(T - rows_done, T - rows_done + SUB) if reverse else slice(rows_done - SUB, rows_done)
        y = ((hf_ref[0, rows, :].astype(F32) + b_buf[rows, :]) * gg_ref[0, rows, :].astype(F32)).astype(BF16)
        ya = jnp.dot(y, wa_ref[...], preferred_element_type=F32)
        m = (ga_ref[0, rows, :].astype(F32) * ya + yb_ref[0, rows, :].astype(F32)).astype(BF16)
        out = jnp.dot(m, wo_ref[...], preferred_element_type=F32)
        xo_ref[0, rows, :] = x_ref[0, rows, :] + mod_ref[0, 2:3, :] * out

    if final:
        h_end = _scan_interleaved(a_buf, b_buf, hcar[...], T, reverse, project)
    else:
        h_end = lax.fori_loop(0, n_blk, body, hcar[...], unroll=4)
    hcar[...] = h_end
    st_ref[0] = h_end[0:1, :]
    if write_h:
        outs[0][0] = b_buf[...].astype(BF16)


def _rnn_call(uc, wg, br, bi, lam, h0, T, *, reverse, final_args=None, write_h=True):
    B, L, D = uc.shape
    nt = L // T
    final = final_args is not None
    tmap = (lambda b, t: (b, nt - 1 - t, 0)) if reverse else (lambda b, t: (b, t, 0))
    tile = pl.BlockSpec((1, T, D), tmap)
    per_batch_h0 = h0.shape[0] > 1
    in_specs = [
        tile,
        _const_spec(wg.shape),
        _const_spec((1, D)),
        _const_spec((1, D)),
        _const_spec((1, D)),
        pl.BlockSpec((1, 1, D), (lambda b, t: (b, 0, 0)) if per_batch_h0 else (lambda b, t: (0, 0, 0))),
    ]
    args = [uc, wg, br, bi, lam, h0]
    out_specs, out_shape = [], []
    if write_h:
        out_specs.append(tile)
        out_shape.append(jax.ShapeDtypeStruct((B, L, D), BF16))
    out_specs.append(pl.BlockSpec((1, 1, D), lambda b, t: (b, 0, 0)))
    out_shape.append(jax.ShapeDtypeStruct((B, 1, D), F32))
    if final:
        hf, gg, ga, yb, x, mod, wa, wo = final_args
        per_batch = mod.shape[0] > 1
        in_specs += [tile, tile, tile, tile, tile,
                     pl.BlockSpec((1, 6, D), (lambda b, t: (b, 0, 0)) if per_batch else (lambda b, t: (0, 0, 0))),
                     _const_spec((D, D)), _const_spec((D, D))]
        args += [hf, gg, ga, yb, x, mod, wa, wo]
        out_specs.append(tile)
        out_shape.append(jax.ShapeDtypeStruct((B, L, D), F32))
    return pl.pallas_call(
        functools.partial(_rnn_kernel, T=T, D=D, reverse=reverse, final=final, write_h=write_h),
        grid=(B, nt),
        in_specs=in_specs,
        out_specs=out_specs,
        out_shape=out_shape,
        scratch_shapes=[pltpu.VMEM((T, D), F32), pltpu.VMEM((T, D), F32), pltpu.VMEM((SUBLANES, D), F32)],
        compiler_params=_params(2),
        name="rnn_bwd_out" if final else ("rnn_bwd" if reverse else "rnn_fwd"),
    )(*args)


def _swiglu_rows(hb, w1_ref, w3_ref, w2_ref, FC):
    F = w1_ref.shape[-1]
    acc = None
    for c in range(F // FC):
        sl = slice(c * FC, (c + 1) * FC)
        a = jnp.dot(hb, w1_ref[:, sl], preferred_element_type=F32)
        b = jnp.dot(hb, w3_ref[:, sl], preferred_element_type=F32)
        act = (a * _sigmoid(a) * b).astype(BF16)
        part = jnp.dot(act, w2_ref[sl, :], preferred_element_type=F32)
        acc = part if acc is None else acc + part
    return acc


def _ffn_kernel(x_ref, mod_ref, g_ref, w1_ref, w3_ref, w2_ref, *rest, FC, final_norm):
    if final_norm:
        fg_ref, o_ref = rest
    else:
        (o_ref,) = rest
    x = x_ref[0]
    hb = _norm_mod(x, g_ref[...], mod_ref[0, 3:4, :], mod_ref[0, 4:5, :]).astype(BF16)
    y = x + mod_ref[0, 5:6, :] * _swiglu_rows(hb, w1_ref, w3_ref, w2_ref, FC)
    if final_norm:
        y = y * lax.rsqrt(jnp.mean(y * y, axis=-1, keepdims=True) + EPS) * fg_ref[...]
    o_ref[0] = y


def _ffn_call(x, mod, g2, w1, w3, w2, T, final_g=None):
    B, L, D = x.shape
    F = w1.shape[-1]
    per_batch = mod.shape[0] > 1
    tile = pl.BlockSpec((1, T, D), lambda b, t: (b, t, 0))
    in_specs = [tile,
                pl.BlockSpec((1, 6, D), (lambda b, t: (b, 0, 0)) if per_batch else (lambda b, t: (0, 0, 0))),
                _const_spec((1, D)), _const_spec((D, F)), _const_spec((D, F)), _const_spec((F, D))]
    args = [x, mod, g2, w1, w3, w2]
    if final_g is not None:
        in_specs.append(_const_spec((1, D)))
        args.append(final_g)
    return pl.pallas_call(
        functools.partial(_ffn_kernel, FC=_ff_chunk(F), final_norm=final_g is not None),
        grid=(B, L // T),
        in_specs=in_specs,
        out_specs=tile,
        out_shape=jax.ShapeDtypeStruct((B, L, D), F32),
        compiler_params=_params(2),
        name="dense_swiglu",
    )(*args)


def _ff_chunk(F):
    for fc in (512, 256, 128):
        if F % fc == 0:
            return fc
    return F


def _route_kernel(x_ref, mod_ref, g_ref, rw_ref, hb_ref, pos_ref, post_ref, comb_ref, cnt_ref, *, T, E):
    h = _norm_mod(x_ref[...], g_ref[...], mod_ref[0, 3:4, :], mod_ref[0, 4:5, :])
    h_hi = h.astype(BF16)
    hb_ref[...] = h_hi
    h_lo = (h - h_hi.astype(F32)).astype(BF16)
    both = jnp.dot(h_hi, rw_ref[...], preferred_element_type=F32)
    logits = (both[:, 0:LANES] + both[:, LANES:2 * LANES]
              + jnp.dot(h_lo, rw_ref[:, 0:LANES], preferred_element_type=F32))
    lane = lax.broadcasted_iota(jnp.int32, logits.shape, 1).astype(F32)
    neg = jnp.float32(-jnp.inf)
    lg = jnp.where(lane < E, logits, neg)
    m1 = jnp.max(lg, axis=-1, keepdims=True)
    i1 = jnp.min(jnp.where(lg == m1, lane, float(LANES)), axis=-1, keepdims=True)
    lg2 = jnp.where(lane == i1, neg, lg)
    m2 = jnp.max(lg2, axis=-1, keepdims=True)
    i2 = jnp.min(jnp.where(lg2 == m2, lane, float(LANES)), axis=-1, keepdims=True)
    e2 = jnp.exp(m2 - m1)
    den = 1.0 + e2
    sel1 = lane == i1
    sel2 = lane == i2
    comb_ref[...] = jnp.where(sel1, 1.0 / den, jnp.where(sel2, e2 / den, 0.0))
    sel = jnp.where(sel1, 1.0, jnp.where(sel2, 1.0, 0.0))
    row = lax.broadcasted_iota(jnp.int32, (T, T), 0)
    col = lax.broadcasted_iota(jnp.int32, (T, T), 1)
    tri = jnp.where(col < row, 1.0, 0.0).astype(BF16)
    rank = jnp.dot(tri, sel.astype(BF16), preferred_element_type=F32)
    pos = jnp.where(sel > 0.0, rank, -1.0)
    pos_ref[...] = pos
    post_ref[0] = jnp.transpose(pos)[0:SUBLANES, :]
    cnt_ref[0] = jnp.sum(sel, axis=0, keepdims=True)


def _route_call(x2, mod, g2, rw, T, tiles_per_batch):
    N, D = x2.shape
    E = rw.shape[1]
    assert E <= SUBLANES
    nT = N // T
    rw_pad = jnp.zeros((D, LANES), F32).at[:, :E].set(rw)
    rw_hi = rw_pad.astype(BF16)
    rw_pad = jnp.concatenate([rw_hi, (rw_pad - rw_hi.astype(F32)).astype(BF16)], axis=1)
    per_batch = mod.shape[0] > 1
    tile = pl.BlockSpec((T, D), lambda j: (j, 0))
    col = pl.BlockSpec((T, LANES), lambda j: (j, 0))
    return pl.pallas_call(
        functools.partial(_route_kernel, T=T, E=E),
        grid=(nT,),
        in_specs=[tile,
                  pl.BlockSpec((1, 6, D), (lambda j: (j // tiles_per_batch, 0, 0)) if per_batch else (lambda j: (0, 0, 0))),
                  _const_spec((1, D)), _const_spec((D, 2 * LANES))],
        out_specs=[tile, col, pl.BlockSpec((1, SUBLANES, T), lambda j: (j, 0, 0)), col,
                   pl.BlockSpec((1, 1, LANES), lambda j: (j, 0, 0))],
        out_shape=[jax.ShapeDtypeStruct((N, D), BF16), jax.ShapeDtypeStruct((N, LANES), F32),
                   jax.ShapeDtypeStruct((nT, SUBLANES, T), F32), jax.ShapeDtypeStruct((N, LANES), F32),
                   jax.ShapeDtypeStruct((nT, 1, LANES), F32)],
        compiler_params=_params(1),
        name="moe_route",
    )(x2, mod, g2, rw_pad)


def _compact_kernel(cnt_ref, start_ref, hb_ref, post_ref, hs_in_ref, hs_ref, buf, sem, busy, *, T, E, CH):
    del hs_in_ref
    j = pl.program_id(0)
    nj = pl.num_programs(0)

    @pl.when(j == 0)
    def _():
        for e in range(E):
            busy[e] = 0

    def copy(e, row):
        return pltpu.make_async_copy(buf.at[e], hs_ref.at[pl.ds(row, CH), :], sem.at[e])

    hbt = hb_ref[...]
    riota = lax.broadcasted_iota(jnp.int32, (CH, T), 0).astype(F32)
    for e in range(E):
        n = cnt_ref[j * E + e]
        s0 = start_ref[j * E + e]
        pe = post_ref[0, e:e + 1, :]

        def chunk(c, carry, e=e, pe=pe, s0=s0):
            base = c * CH

            @pl.when(busy[e] == 1)
            def _():
                copy(e, 0).wait()

            onehot = jnp.where(pe == riota + base.astype(F32), 1.0, 0.0).astype(BF16)
            buf[e] = jnp.dot(onehot, hbt, preferred_element_type=F32).astype(BF16)
            copy(e, pl.multiple_of(s0 + base, BF16_ROWS)).start()
            busy[e] = 1
            return carry

        lax.fori_loop(0, (n + CH - 1) // CH, chunk, 0)

    @pl.when(j == nj - 1)
    def _():
        for e in range(E):
            @pl.when(busy[e] == 1)
            def _():
                copy(e, 0).wait()


def _compact_call(cnt, start, hb, post, hs0, T, E, CH):
    N, D = hb.shape
    nT = N // T
    grid_spec = pltpu.PrefetchScalarGridSpec(
        num_scalar_prefetch=2,
        grid=(nT,),
        in_specs=[pl.BlockSpec((T, D), lambda j, *_: (j, 0)),
                  pl.BlockSpec((1, SUBLANES, T), lambda j, *_: (j, 0, 0)),
                  pl.BlockSpec(memory_space=pl.ANY)],
        out_specs=pl.BlockSpec(memory_space=pl.ANY),
        scratch_shapes=[pltpu.VMEM((E, CH, D), BF16), pltpu.SemaphoreType.DMA((E,)), pltpu.SMEM((E,), jnp.int32)],
    )
    return pl.pallas_call(
        functools.partial(_compact_kernel, T=T, E=E, CH=CH),
        grid_spec=grid_spec,
        out_shape=jax.ShapeDtypeStruct(hs0.shape, BF16),
        input_output_aliases={4: 0},
        compiler_params=_params(1),
        name="moe_compact",
    )(cnt, start, hb, post, hs0)


def _gffn_kernel(te_ref, rows_ref, hs_ref, w1_ref, w3_ref, w2_ref, ys_ref, *, FC, TM):
    del te_ref
    n = rows_ref[pl.program_id(0)]
    half = TM // 2
    w = (w1_ref.at[0], w3_ref.at[0], w2_ref.at[0])

    @pl.when(n > half)
    def _():
        ys_ref[...] = _swiglu_rows(hs_ref[...], *w, FC).astype(BF16)

    @pl.when((n > 0) & (n <= half))
    def _():
        ys_ref[0:half, :] = _swiglu_rows(hs_ref[0:half, :], *w, FC).astype(BF16)
        ys_ref[half:TM, :] = jnp.zeros((TM - half, ys_ref.shape[1]), BF16)

    @pl.when(n == 0)
    def _():
        ys_ref[...] = jnp.zeros(ys_ref.shape, BF16)


def _gffn_call(tile_expert, tile_rows, hs, w1, w3, w2, TM):
    R, D = hs.shape
    E, _, F = w1.shape
    grid_spec = pltpu.PrefetchScalarGridSpec(
        num_scalar_prefetch=2,
        grid=(R // TM,),
        in_specs=[pl.BlockSpec((TM, D), lambda i, te, va: (i, 0)),
                  pl.BlockSpec((1, D, F), lambda i, te, va: (te[i], 0, 0)),
                  pl.BlockSpec((1, D, F), lambda i, te, va: (te[i], 0, 0)),
                  pl.BlockSpec((1, F, D), lambda i, te, va: (te[i], 0, 0))],
        out_specs=pl.BlockSpec((TM, D), lambda i, te, va: (i, 0)),
    )
    return pl.pallas_call(
        functools.partial(_gffn_kernel, FC=_ff_chunk(F), TM=TM),
        grid_spec=grid_spec,
        out_shape=jax.ShapeDtypeStruct((R, D), BF16),
        compiler_params=_params(1),
        name="moe_grouped_swiglu",
    )(tile_expert, tile_rows, hs, w1, w3, w2)


def _combine_kernel(cnt_ref, start_ref, x_ref, mod_ref, pos_ref, comb_ref, ys_ref, *rest, T, E, CH, final_norm):
    if final_norm:
        fg_ref, o_ref, ybuf, xbuf, acc, sem, xsem = rest
    else:
        o_ref, ybuf, xbuf, acc, sem, xsem = rest
    j = pl.program_id(0)
    nj = pl.num_programs(0)
    slot = j % 2

    def first_chunk(jj, sl, e):
        row = pl.multiple_of(start_ref[jj * E + e], BF16_ROWS)
        return pltpu.make_async_copy(ys_ref.at[pl.ds(row, CH), :], ybuf.at[sl, e], sem.at[sl, e])

    @pl.when(j == 0)
    def _():
        for e in range(E):
            first_chunk(0, 0, e).start()

    @pl.when(j + 1 < nj)
    def _():
        for e in range(E):
            first_chunk(j + 1, 1 - slot, e).start()

    liota = lax.broadcasted_iota(jnp.int32, (T, CH), 1).astype(F32)
    for e in range(E):
        n = cnt_ref[j * E + e]
        s0 = start_ref[j * E + e]
        pe = pos_ref[:, e:e + 1]
        we = comb_ref[:, e:e + 1]
        first_chunk(j, slot, e).wait()
        onehot = jnp.where(pe == liota, 1.0, 0.0).astype(BF16)
        part = we * jnp.dot(onehot, ybuf[slot, e], preferred_element_type=F32)
        if e == 0:
            acc[...] = part
        else:
            acc[...] += part

        def extra(c, carry, pe=pe, we=we, s0=s0):
            base = c * CH
            cp = pltpu.make_async_copy(ys_ref.at[pl.ds(pl.multiple_of(s0 + base, BF16_ROWS), CH), :], xbuf, xsem)
            cp.start()
            cp.wait()
            onehot_c = jnp.where(pe == liota + base.astype(F32), 1.0, 0.0).astype(BF16)
            acc[...] += we * jnp.dot(onehot_c, xbuf[...], preferred_element_type=F32)
            return carry

        lax.fori_loop(1, (n + CH - 1) // CH, extra, 0)
    y = x_ref[...] + mod_ref[0, 5:6, :] * acc[...]
    if final_norm:
        y = y * lax.rsqrt(jnp.mean(y * y, axis=-1, keepdims=True) + EPS) * fg_ref[...]
    o_ref[...] = y


def _combine_call(cnt, start, x2, mod, pos, comb, ys, T, E, CH, tiles_per_batch, final_g=None):
    N, D = x2.shape
    per_batch = mod.shape[0] > 1
    tile = pl.BlockSpec((T, D), lambda j, *_: (j, 0))
    col = pl.BlockSpec((T, LANES), lambda j, *_: (j, 0))
    in_specs = [tile,
                pl.BlockSpec((1, 6, D), (lambda j, *_: (j // tiles_per_batch, 0, 0)) if per_batch
                             else (lambda j, *_: (0, 0, 0))),
                col, col, pl.BlockSpec(memory_space=pl.ANY)]
    args = [cnt, start, x2, mod, pos, comb, ys]
    if final_g is not None:
        in_specs.append(pl.BlockSpec((1, D), lambda j, *_: (0, 0)))
        args.append(final_g)
    grid_spec = pltpu.PrefetchScalarGridSpec(
        num_scalar_prefetch=2,
        grid=(N // T,),
        in_specs=in_specs,
        out_specs=tile,
        scratch_shapes=[pltpu.VMEM((2, E, CH, D), BF16), pltpu.VMEM((CH, D), BF16), pltpu.VMEM((T, D), F32),
                        pltpu.SemaphoreType.DMA((2, E)), pltpu.SemaphoreType.DMA(())],
    )
    return pl.pallas_call(
        functools.partial(_combine_kernel, T=T, E=E, CH=CH, final_norm=final_g is not None),
        grid_spec=grid_spec,
        out_shape=jax.ShapeDtypeStruct((N, D), F32),
        compiler_params=_params(1),
        name="moe_combine",
    )(*args)


def _moe(x, mod, g2, rw, w1, w3, w2, final_g=None):
    B, L, D = x.shape
    N = B * L
    E = rw.shape[1]
    T = min(512, L)
    CH = min(256, T)
    TM = 512
    nT = N // T
    x2 = x.reshape(N, D)
    hb, pos, post, comb, cnt_f = _route_call(x2, mod, g2, rw, T, L // T)

    cnt = cnt_f[:, 0, :E].astype(jnp.int32)
    npad = (cnt + BF16_ROWS - 1) // BF16_ROWS * BF16_ROWS
    glen = (npad.sum(0) + CH + TM - 1) // TM * TM
    gend = jnp.cumsum(glen)
    start = (gend - glen)[None, :] + jnp.cumsum(npad, axis=0) - npad
    rows_bound = 2 * N + nT * E * BF16_ROWS + E * (CH + TM)
    R = (rows_bound + TM - 1) // TM * TM
    tile_row = jnp.arange(R // TM, dtype=jnp.int32) * TM
    tile_expert = jnp.minimum(jnp.sum(tile_row[:, None] >= gend[None, :], axis=1), E - 1).astype(jnp.int32)
    data_end = (gend - glen + npad.sum(0))[tile_expert]
    tile_rows = jnp.clip(data_end - tile_row, 0, TM).astype(jnp.int32)
    cnt1, start1 = cnt.reshape(-1), start.reshape(-1).astype(jnp.int32)

    hs = _compact_call(cnt1, start1, hb, post, jnp.zeros((R, D), BF16), T, E, CH)
    ys = _gffn_call(tile_expert, tile_rows, hs, w1, w3, w2, TM)
    out = _combine_call(cnt1, start1, x2, mod, pos, comb, ys, T, E, CH, L // T, final_g)
    return out.reshape(B, L, D)


def _pack_gates(wr, wi):
    H, hd, _ = wr.shape
    per = GATE_GROUP // hd
    G = H // per

    def blockdiag(w):
        w = w.reshape(G, per, hd, hd)
        eye = jnp.eye(per, dtype=w.dtype)
        return jnp.einsum('gpij,pq->gpiqj', w, eye).reshape(G, GATE_GROUP, GATE_GROUP)

    return (0.5 * jnp.concatenate([blockdiag(wr), blockdiag(wi)], axis=-1)).astype(BF16)


def _mixer(x, mod, p, stream_tiles, conv_stride, h0_f, h0_b, full):
    T_in, T_rnn, T_cv, HB = stream_tiles
    uc, gg, v, ga, gb = _inproj_call(x, mod, p['g1'], p['w_in'], p['rnn_cw'], p['rnn_cb'], T_in)
    rnn_f = (uc, p['wg'][0], p['br'][0], p['bi'][0], p['lam'][0], h0_f, T_rnn)
    rnn_b = (uc, p['wg'][1], p['br'][1], p['bi'][1], p['lam'][1], h0_b, T_rnn)
    if not full:
        (s_f,) = _rnn_call(*rnn_f, reverse=False, write_h=False)
        (s_b,) = _rnn_call(*rnn_b, reverse=True, write_h=False)
        return None, s_f, s_b
    ybg = _convb_call(v, gb, p['cw'], p['cb'], p['lg'], p['lb'], p['w_b'], T_cv, HB, conv_stride)
    hf, s_f = _rnn_call(*rnn_f, reverse=False)
    s_b, x_new = _rnn_call(*rnn_b, reverse=True, write_h=False,
                           final_args=(hf, gg, ga, ybg, x, mod, p['w_a'], p['w_out']))
    return x_new, s_f, s_b


def kernel(x, c, ctx, c_ctx, mod_w, mod_b, norm1_g, norm2_g, w_in, rnn_conv_w, rnn_conv_b, lru_wr, lru_br, lru_wi, lru_bi, lru_lam, conv_w, conv_b, conv_ln_g, conv_ln_b, w_branch_a, w_branch_b, w_out, ffn_w1, ffn_w3, ffn_w2, moe_router, moe_w1, moe_w3, moe_w2, final_g):
    B, L, D = x.shape
    Lc = ctx.shape[1]
    depth = mod_w.shape[0]
    assert B + 1 <= SUBLANES and D % GATE_GROUP == 0 and GATE_GROUP % lru_wr.shape[-1] == 0

    rows = jnp.zeros((SUBLANES, D), F32).at[:B].set(c).at[B].set(c_ctx)
    mod_all = _mod_call(rows, mod_w, mod_b)

    lat_tiles = (min(512, L), min(512, L), min(1024, L), min(1024, L))
    ctx_tiles = (Lc, Lc, Lc, BF16_ROWS)
    T_ffn = min(512, L)
    zero_state = jnp.zeros((1, 1, D), F32)
    fg = final_g.reshape(1, D)
    in_col_scale = jnp.repeat(jnp.array([1.0, 1.0, 0.5, 0.5, 0.5, 0.5], F32), D)[None, :]

    for i in range(depth):
        last = i == depth - 1
        mod_x = mod_all[i, :B].reshape(B, 6, D)
        mod_c = mod_all[i, B].reshape(1, 6, D)
        p = dict(
            g1=norm1_g[i].reshape(1, D), w_in=(w_in[i] * in_col_scale).astype(BF16),
            rnn_cw=rnn_conv_w[i], rnn_cb=rnn_conv_b[i].reshape(1, D),
            wg=[_pack_gates(lru_wr[i, d], lru_wi[i, d]) for d in range(2)],
            br=[0.5 * lru_br[i, d].reshape(1, D) for d in range(2)],
            bi=[0.5 * lru_bi[i, d].reshape(1, D) for d in range(2)],
            lam=[lru_lam[i, d].reshape(1, D) for d in range(2)],
            cw=conv_w[i], cb=conv_b[i].reshape(1, D), lg=conv_ln_g[i].reshape(1, D), lb=conv_ln_b[i].reshape(1, D),
            w_a=w_branch_a[i].astype(BF16), w_b=w_branch_b[i].astype(BF16), w_out=w_out[i].astype(BF16))
        g2 = norm2_g[i].reshape(1, D)
        j = i // 2
        dense = i % 2 == 0
        if dense:
            ffn_w = (ffn_w1[j].astype(BF16), ffn_w3[j].astype(BF16), ffn_w2[j].astype(BF16))
        else:
            ffn_w = (moe_router[j], moe_w1[j].astype(BF16), moe_w3[j].astype(BF16), moe_w2[j].astype(BF16))

        ctx_new, s_f, s_b = _mixer(ctx, mod_c, p, ctx_tiles, 1, zero_state, zero_state, full=not last)
        if not last:
            ctx = (_ffn_call(ctx_new, mod_c, g2, *ffn_w, Lc) if dense
                   else _moe(ctx_new, mod_c, g2, *ffn_w))
        x, _, _ = _mixer(x, mod_x, p, lat_tiles, GRID_W, s_f, s_b, full=True)
        fgi = fg if last else None
        x = (_ffn_call(x, mod_x, g2, *ffn_w, T_ffn, fgi) if dense else _moe(x, mod_x, g2, *ffn_w, fgi))
    return x
```

```python
import functools

import jax
import jax.numpy as jnp
from jax import lax
from jax.experimental import pallas as pl
from jax.experimental.pallas import tpu as pltpu

EPS = 1e-6
LRU_C = 8.0
LOG2_E = 1.4426950408889634
GRID_W = 64
RNN_PAD_L = 2
GATE_GROUP = 256
SUBLANES = 8
LANES = 128
BF16_ROWS = 16
SCAN_SUB = 128
PROJ_COLS = 256
VMEM_LIMIT = 56 * 1024 * 1024

F32 = jnp.float32
BF16 = jnp.bfloat16


def _sigmoid(x):
    return 0.5 * jnp.tanh(0.5 * x) + 0.5


def _gelu_tanh(x):
    return 0.5 * x * (1.0 + jnp.tanh(0.7978845608028654 * (x + 0.044715 * (x * x * x))))


def _norm_mod(x, g, shift, scale):
    ms = jnp.mean(x * x, axis=-1, keepdims=True)
    y = x * lax.rsqrt(ms + EPS) * g
    return y * (1.0 + scale) + shift


def _params(n_axes):
    return pltpu.CompilerParams(dimension_semantics=("arbitrary",) * n_axes, vmem_limit_bytes=VMEM_LIMIT)


def _const_spec(shape):
    nd = len(shape)
    return pl.BlockSpec(shape, lambda *_: (0,) * nd, pipeline_mode=pl.Buffered(1))


def _lru_coefficients(read_u, wg_ref, br_ref, bi_ref, lam_ref, a_buf, b_buf, D):
    z = -lam_ref[...]
    clam = -LRU_C * (jnp.maximum(z, 0.0) + jnp.log1p(jnp.exp(-jnp.abs(z))))
    c2 = (0.5 * LOG2_E) * clam
    GW = GATE_GROUP
    for g in range(D // GW):
        sl = slice(g * GW, (g + 1) * GW)
        ug = read_u(sl)
        pre = jnp.dot(ug, wg_ref[g], preferred_element_type=F32)
        tr = jnp.tanh(pre[:, :GW] + br_ref[:, sl])
        ti = jnp.tanh(pre[:, GW:] + bi_ref[:, sl])
        a = jnp.exp2(c2[:, sl] * tr + c2[:, sl])
        a_buf[:, sl] = a
        q = 1.0 - a * a
        root = jnp.where(q > 0.0, q * lax.rsqrt(q), 0.0)
        uh = 0.5 * ug.astype(F32)
        b_buf[:, sl] = root * (ti * uh + uh)


def _scan_rows(a, b, h, reverse):
    sub = lax.broadcasted_iota(jnp.int32, a.shape, 0)
    half = SUBLANES // 2
    for k in (1, 2):
        if reverse:
            ar, br = pltpu.roll(a, SUBLANES - k, 0), pltpu.roll(b, SUBLANES - k, 0)
            m = (sub % half) < half - k
        else:
            ar, br = pltpu.roll(a, k, 0), pltpu.roll(b, k, 0)
            m = (sub % half) >= k
        b = jnp.where(m, a * br + b, b)
        a = jnp.where(m, a * ar, a)
    first = b + a * h
    mid = first[half:half + 1, :] if reverse else first[half - 1:half, :]
    second = b + a * jnp.broadcast_to(mid, a.shape)
    hh = jnp.where((sub >= half) if reverse else (sub < half), first, second)
    edge = hh[0:1, :] if reverse else hh[SUBLANES - 1:SUBLANES, :]
    return hh, jnp.broadcast_to(edge, hh.shape)


def _scan_interleaved(a_buf, b_buf, h, T, reverse, mxu_work):
    n_blk = T // SUBLANES
    mxu_work(0)
    for i in range(n_blk):
        r0 = (n_blk - 1 - i if reverse else i) * SUBLANES
        hh, h = _scan_rows(a_buf[r0:r0 + SUBLANES, :], b_buf[r0:r0 + SUBLANES, :], h, reverse)
        b_buf[r0:r0 + SUBLANES, :] = hh
        mxu_work(i + 1)
    return h


def _mod_kernel(rows_ref, w_ref, b_ref, o_ref):
    r = rows_ref[...]
    s = r * _sigmoid(r)
    o_ref[0] = jnp.dot(s, w_ref[0], precision=lax.Precision.HIGHEST, preferred_element_type=F32) + b_ref[0]


def _mod_call(rows, mod_w, mod_b):
    depth, d, n6 = mod_w.shape
    tn = d
    return pl.pallas_call(
        _mod_kernel,
        grid=(depth, n6 // tn),
        in_specs=[
            pl.BlockSpec((SUBLANES, d), lambda l, n: (0, 0)),
            pl.BlockSpec((1, d, tn), lambda l, n: (l, 0, n)),
            pl.BlockSpec((1, 1, tn), lambda l, n: (l, 0, n)),
        ],
        out_specs=pl.BlockSpec((1, SUBLANES, tn), lambda l, n: (l, 0, n)),
        out_shape=jax.ShapeDtypeStruct((depth, SUBLANES, n6), F32),
        compiler_params=_params(2),
        name="adaln_mod",
    )(rows, mod_w, mod_b.reshape(depth, 1, n6))


def _inproj_kernel(x_ref, xp_ref, xn_ref, mod_ref, g_ref, w_ref, cw_ref, cb_ref,
                   uc_ref, gg_ref, v_ref, ga_ref, gb_ref, zbuf, *, T, D, KW):
    t = pl.program_id(1)
    nt = pl.num_programs(1)
    shift = mod_ref[0, 0:1, :]
    scale = mod_ref[0, 1:2, :]
    g = g_ref[...]
    h = _norm_mod(x_ref[0], g, shift, scale).astype(BF16)
    hp = _norm_mod(xp_ref[0], g, shift, scale).astype(BF16)
    hn = _norm_mod(xn_ref[0], g, shift, scale).astype(BF16)

    w0 = w_ref[:, 0:D]
    zbuf[0:SUBLANES, :] = jnp.dot(hp, w0, preferred_element_type=F32) * (t > 0).astype(F32)
    zbuf[SUBLANES:SUBLANES + T, :] = jnp.dot(h, w0, preferred_element_type=F32)
    zbuf[SUBLANES + T:2 * SUBLANES + T, :] = jnp.dot(hn, w0, preferred_element_type=F32) * (t < nt - 1).astype(F32)
    uc = jnp.broadcast_to(cb_ref[...], (T, D))
    for k in range(KW):
        uc = uc + cw_ref[k:k + 1, :] * zbuf[SUBLANES - RNN_PAD_L + k:SUBLANES - RNN_PAD_L + k + T, :]
    uc_ref[0] = uc.astype(BF16)

    def proj(group, c):
        lo = group * D + c * PROJ_COLS
        return jnp.dot(h, w_ref[:, lo:lo + PROJ_COLS], preferred_element_type=F32)

    for c in range(D // PROJ_COLS):
        cols = slice(c * PROJ_COLS, (c + 1) * PROJ_COLS)
        gg_ref[0, :, cols] = _gelu_tanh(proj(1, c)).astype(BF16)
        vh = proj(2, c)
        v_ref[0, :, cols] = (vh * jnp.tanh(proj(3, c)) + vh).astype(BF16)
        ga_ref[0, :, cols] = (0.5 * jnp.tanh(proj(4, c)) + 0.5).astype(BF16)
        gb_ref[0, :, cols] = (0.5 * jnp.tanh(proj(5, c)) + 0.5).astype(BF16)


def _inproj_call(x, mod, g1, w_in, cw, cb, T):
    B, L, D = x.shape
    KW = cw.shape[0]
    nt = L // T
    per_batch = mod.shape[0] > 1
    tb = T // SUBLANES
    last = L // SUBLANES - 1
    tile = pl.BlockSpec((1, T, D), lambda b, t: (b, t, 0))
    out_sds = jax.ShapeDtypeStruct((B, L, D), BF16)
    return pl.pallas_call(
        functools.partial(_inproj_kernel, T=T, D=D, KW=KW),
        grid=(B, nt),
        in_specs=[
            tile,
            pl.BlockSpec((1, SUBLANES, D), lambda b, t: (b, jnp.maximum(t * tb - 1, 0), 0)),
            pl.BlockSpec((1, SUBLANES, D), lambda b, t: (b, jnp.minimum((t + 1) * tb, last), 0)),
            pl.BlockSpec((1, 6, D), (lambda b, t: (b, 0, 0)) if per_batch else (lambda b, t: (0, 0, 0))),
            _const_spec((1, D)),
            _const_spec(w_in.shape),
            _const_spec((KW, D)),
            _const_spec((1, D)),
        ],
        out_specs=[tile] * 5,
        out_shape=[out_sds] * 5,
        scratch_shapes=[pltpu.VMEM((T + 2 * SUBLANES, D), F32)],
        compiler_params=_params(2),
        name="in_projection",
    )(x, x, x, mod, g1, w_in, cw, cb)


def _convb_kernel(v_ref, vp_ref, vn_ref, cw_ref, cb_ref, lg_ref, lb_ref, wb_ref, gmb_ref, o_ref,
                  vbuf, cbuf, ybuf, *, T, D, HB, S, RC, KW):
    t = pl.program_id(1)
    nt = pl.num_programs(1)
    ring = S % SUBLANES == 0 and HB == T
    if ring:
        @pl.when(t == 0)
        def _():
            vbuf[0:T, :] = v_ref[0].astype(F32)
            vbuf[2 * T:3 * T, :] = jnp.zeros((T, D), F32)

        next_rows = pl.ds(pl.multiple_of(((t + 1) % 3) * T, T), T)

        @pl.when(t < nt - 1)
        def _():
            vbuf[next_rows, :] = vn_ref[0].astype(F32)

        @pl.when(t == nt - 1)
        def _():
            vbuf[next_rows, :] = jnp.zeros((T, D), F32)

        ring_base = (t % 3 + 3) * T
    else:
        vbuf[0:HB, :] = vp_ref[0].astype(F32) * (t > 0).astype(F32)
        vbuf[HB:HB + T, :] = v_ref[0].astype(F32)
        vbuf[HB + T:2 * HB + T, :] = vn_ref[0].astype(F32) * (t < nt - 1).astype(F32)
    pad = KW // 2
    lg_half = 0.5 * lg_ref[...]
    lb_half = 0.5 * lb_ref[...]

    def norm_act(r0, rows):
        x = cbuf[pl.ds(r0, rows), :]
        mu = jnp.mean(x, axis=-1, keepdims=True)
        xc = x - mu
        var = jnp.mean(xc * xc, axis=-1, keepdims=True)
        yh = xc * lax.rsqrt(var + EPS) * lg_half + lb_half
        ybuf[pl.ds(r0, rows), :] = (yh * jnp.tanh(yh) + yh).astype(BF16)

    if S % SUBLANES == 0:
        CR = 2 * S

        def conv_rows(i, carry):
            r0 = pl.multiple_of(i * CR, CR)
            if ring:
                starts = [pl.multiple_of(lax.rem(ring_base + r0 + (p - pad) * S, 3 * T), S) for p in range(KW + 1)]
            else:
                starts = [r0 + (HB + (p - pad) * S) for p in range(KW + 1)]
            for lc in range(D // LANES):
                ln = slice(lc * LANES, (lc + 1) * LANES)
                acc0 = jnp.broadcast_to(cb_ref[:, ln], (S, LANES))
                acc1 = acc0
                for p in range(KW + 1):
                    piece = vbuf[pl.ds(starts[p], S), ln]
                    if p < KW:
                        acc0 = acc0 + cw_ref[p:p + 1, ln] * piece
                    if p >= 1:
                        acc1 = acc1 + cw_ref[p - 1:p, ln] * piece
                cbuf[pl.ds(r0, S), ln] = acc0
                cbuf[pl.ds(r0 + S, S), ln] = acc1
            norm_act(r0, CR)
            return carry

        lax.fori_loop(0, T // CR, conv_rows, 0)
    else:
        for i in range(T // RC):
            acc = jnp.broadcast_to(cb_ref[...], (RC, D))
            for k in range(KW):
                acc = acc + cw_ref[k:k + 1, :] * vbuf[i * RC + HB + (k - pad) * S:i * RC + HB + (k - pad) * S + RC, :]
            cbuf[i * RC:(i + 1) * RC, :] = acc
            norm_act(i * RC, RC)
    yb = jnp.dot(ybuf[...], wb_ref[...], preferred_element_type=F32)
    o_ref[0] = (yb * gmb_ref[0].astype(F32)).astype(BF16)


def _convb_call(v, gmb, cw, cb, lg, lb, wb, T, HB, S):
    B, L, D = v.shape
    KW = cw.shape[0]
    assert (KW // 2) * S <= HB and T % HB == 0 and L % T == 0 and T % (2 * S) == 0
    nt = L // T
    r = T // HB
    last = L // HB - 1
    tile = pl.BlockSpec((1, T, D), lambda b, t: (b, t, 0))
    return pl.pallas_call(
        functools.partial(_convb_kernel, T=T, D=D, HB=HB, S=S, RC=BF16_ROWS, KW=KW),
        grid=(B, nt),
        in_specs=[
            tile,
            pl.BlockSpec((1, HB, D), lambda b, t: (b, jnp.maximum(t * r - 1, 0), 0)),
            pl.BlockSpec((1, HB, D), lambda b, t: (b, jnp.minimum((t + 1) * r, last), 0)),
            _const_spec(cw.shape),
            _const_spec((1, D)),
            _const_spec((1, D)),
            _const_spec((1, D)),
            _const_spec((D, D)),
            tile,
        ],
        out_specs=tile,
        out_shape=jax.ShapeDtypeStruct((B, L, D), BF16),
        scratch_shapes=[pltpu.VMEM((T + 2 * HB, D), F32), pltpu.VMEM((T, D), F32), pltpu.VMEM((T, D), BF16)],
        compiler_params=_params(2),
        name="conv_branch",
    )(v, v, v, cw, cb, lg, lb, wb, gmb)


def _rnn_kernel(*refs, T, D, reverse, final, write_h):
    uc_ref, wg_ref, br_ref, bi_ref, lam_ref, h0_ref = refs[:6]
    pos = 6
    if final:
        hf_ref, gg_ref, ga_ref, yb_ref, x_ref, mod_ref, wa_ref, wo_ref = refs[pos:pos + 8]
        pos += 8
    outs = []
    if write_h:
        outs.append(refs[pos]); pos += 1
    st_ref = refs[pos]; pos += 1
    if final:
        xo_ref = refs[pos]; pos += 1
    a_buf, b_buf, hcar = refs[pos:pos + 3]

    t = pl.program_id(1)

    @pl.when(t == 0)
    def _():
        hcar[...] = jnp.broadcast_to(h0_ref[0], (SUBLANES, D))

    _lru_coefficients(lambda sl: uc_ref[0, :, sl], wg_ref, br_ref, bi_ref, lam_ref, a_buf, b_buf, D)

    n_blk = T // SUBLANES

    def body(i, h):
        blk = (n_blk - 1 - i) if reverse else i
        rows = pl.ds(pl.multiple_of(blk * SUBLANES, SUBLANES), SUBLANES)
        hh, h_next = _scan_rows(a_buf[rows, :], b_buf[rows, :], h, reverse)
        b_buf[rows, :] = hh
        return h_next

    SUB = min(SCAN_SUB, T)

    def project(n_done):
        rows_done = n_done * SUBLANES
        if rows_done == 0 or rows_done % SUB:
            return
        rows = slice(T - rows_done, T - rows_done + SUB) if reverse else slice(rows_done - SUB, rows_done)
        y = ((hf_ref[0, rows, :].astype(F32) + b_buf[rows, :]) * gg_ref[0, rows, :].astype(F32)).astype(BF16)
        ya = jnp.dot(y, wa_ref[...], preferred_element_type=F32)
        m = (ga_ref[0, rows, :].astype(F32) * ya + yb_ref[0, rows, :].astype(F32)).astype(BF16)
        out = jnp.dot(m, wo_ref[...], preferred_element_type=F32)
        xo_ref[0, rows, :] = x_ref[0, rows, :] + mod_ref[0, 2:3, :] * out

    if final:
        h_end = _scan_interleaved(a_buf, b_buf, hcar[...], T, reverse, project)
    else:
        h_end = lax.fori_loop(0, n_blk, body, hcar[...], unroll=4)
    hcar[...] = h_end
    st_ref[0] = h_end[0:1, :]
    if write_h:
        outs[0][0] = b_buf[...].astype(BF16)


def _rnn_call(uc, wg, br, bi, lam, h0, T, *, reverse, final_args=None, write_h=True):
    B, L, D = uc.shape
    nt = L // T
    final = final_args is not None
    tmap = (lambda b, t: (b, nt - 1 - t, 0)) if reverse else (lambda b, t: (b, t, 0))
    tile = pl.BlockSpec((1, T, D), tmap)
    per_batch_h0 = h0.shape[0] > 1
    in_specs = [
        tile,
        _const_spec(wg.shape),
        _const_spec((1, D)),
        _const_spec((1, D)),
        _const_spec((1, D)),
        pl.BlockSpec((1, 1, D), (lambda b, t: (b, 0, 0)) if per_batch_h0 else (lambda b, t: (0, 0, 0))),
    ]
    args = [uc, wg, br, bi, lam, h0]
    out_specs, out_shape = [], []
    if write_h:
        out_specs.append(tile)
        out_shape.append(jax.ShapeDtypeStruct((B, L, D), BF16))
    out_specs.append(pl.BlockSpec((1, 1, D), lambda b, t: (b, 0, 0)))
    out_shape.append(jax.ShapeDtypeStruct((B, 1, D), F32))
    if final:
        hf, gg, ga, yb, x, mod, wa, wo = final_args
        per_batch = mod.shape[0] > 1
        in_specs += [tile, tile, tile, tile, tile,
                     pl.BlockSpec((1, 6, D), (lambda b, t: (b, 0, 0)) if per_batch else (lambda b, t: (0, 0, 0))),
                     _const_spec((D, D)), _const_spec((D, D))]
        args += [hf, gg, ga, yb, x, mod, wa, wo]
        out_specs.append(tile)
        out_shape.append(jax.ShapeDtypeStruct((B, L, D), F32))
    return pl.pallas_call(
        functools.partial(_rnn_kernel, T=T, D=D, reverse=reverse, final=final, write_h=write_h),
        grid=(B, nt),
        in_specs=in_specs,
        out_specs=out_specs,
        out_shape=out_shape,
        scratch_shapes=[pltpu.VMEM((T, D), F32), pltpu.VMEM((T, D), F32), pltpu.VMEM((SUBLANES, D), F32)],
        compiler_params=_params(2),
        name="rnn_bwd_out" if final else ("rnn_bwd" if reverse else "rnn_fwd"),
    )(*args)


def _swiglu_rows(hb, w1_ref, w3_ref, w2_ref, FC):
    F = w1_ref.shape[-1]
    acc = None
    for c in range(F // FC):
        sl = slice(c * FC, (c + 1) * FC)
        a = jnp.dot(hb, w1_ref[:, sl], preferred_element_type=F32)
        b = jnp.dot(hb, w3_ref[:, sl], preferred_element_type=F32)
        act = (a * _sigmoid(a) * b).astype(BF16)
        part = jnp.dot(act, w2_ref[sl, :], preferred_element_type=F32)
        acc = part if acc is None else acc + part
    return acc


def _ffn_kernel(x_ref, mod_ref, g_ref, w1_ref, w3_ref, w2_ref, *rest, FC, final_norm):
    if final_norm:
        fg_ref, o_ref = rest
    else:
        (o_ref,) = rest
    x = x_ref[0]
    hb = _norm_mod(x, g_ref[...], mod_ref[0, 3:4, :], mod_ref[0, 4:5, :]).astype(BF16)
    y = x + mod_ref[0, 5:6, :] * _swiglu_rows(hb, w1_ref, w3_ref, w2_ref, FC)
    if final_norm:
        y = y * lax.rsqrt(jnp.mean(y * y, axis=-1, keepdims=True) + EPS) * fg_ref[...]
    o_ref[0] = y


def _ffn_call(x, mod, g2, w1, w3, w2, T, final_g=None):
    B, L, D = x.shape
    F = w1.shape[-1]
    per_batch = mod.shape[0] > 1
    tile = pl.BlockSpec((1, T, D), lambda b, t: (b, t, 0))
    in_specs = [tile,
                pl.BlockSpec((1, 6, D), (lambda b, t: (b, 0, 0)) if per_batch else (lambda b, t: (0, 0, 0))),
                _const_spec((1, D)), _const_spec((D, F)), _const_spec((D, F)), _const_spec((F, D))]
    args = [x, mod, g2, w1, w3, w2]
    if final_g is not None:
        in_specs.append(_const_spec((1, D)))
        args.append(final_g)
    return pl.pallas_call(
        functools.partial(_ffn_kernel, FC=_ff_chunk(F), final_norm=final_g is not None),
        grid=(B, L // T),
        in_specs=in_specs,
        out_specs=tile,
        out_shape=jax.ShapeDtypeStruct((B, L, D), F32),
        compiler_params=_params(2),
        name="dense_swiglu",
    )(*args)


def _ff_chunk(F):
    for fc in (512, 256, 128):
        if F % fc == 0:
            return fc
    return F


def _route_kernel(x_ref, mod_ref, g_ref, rw_ref, hb_ref, pos_ref, post_ref, comb_ref, cnt_ref, *, T, E):
    h = _norm_mod(x_ref[...], g_ref[...], mod_ref[0, 3:4, :], mod_ref[0, 4:5, :])
    h_hi = h.astype(BF16)
    hb_ref[...] = h_hi
    h_lo = (h - h_hi.astype(F32)).astype(BF16)
    both = jnp.dot(h_hi, rw_ref[...], preferred_element_type=F32)
    logits = (both[:, 0:LANES] + both[:, LANES:2 * LANES]
              + jnp.dot(h_lo, rw_ref[:, 0:LANES], preferred_element_type=F32))
    lane = lax.broadcasted_iota(jnp.int32, logits.shape, 1).astype(F32)
    neg = jnp.float32(-jnp.inf)
    lg = jnp.where(lane < E, logits, neg)
    m1 = jnp.max(lg, axis=-1, keepdims=True)
    i1 = jnp.min(jnp.where(lg == m1, lane, float(LANES)), axis=-1, keepdims=True)
    lg2 = jnp.where(lane == i1, neg, lg)
    m2 = jnp.max(lg2, axis=-1, keepdims=True)
    i2 = jnp.min(jnp.where(lg2 == m2, lane, float(LANES)), axis=-1, keepdims=True)
    e2 = jnp.exp(m2 - m1)
    den = 1.0 + e2
    sel1 = lane == i1
    sel2 = lane == i2
    comb_ref[...] = jnp.where(sel1, 1.0 / den, jnp.where(sel2, e2 / den, 0.0))
    sel = jnp.where(sel1, 1.0, jnp.where(sel2, 1.0, 0.0))
    row = lax.broadcasted_iota(jnp.int32, (T, T), 0)
    col = lax.broadcasted_iota(jnp.int32, (T, T), 1)
    tri = jnp.where(col < row, 1.0, 0.0).astype(BF16)
    rank = jnp.dot(tri, sel.astype(BF16), preferred_element_type=F32)
    pos = jnp.where(sel > 0.0, rank, -1.0)
    pos_ref[...] = pos
    post_ref[0] = jnp.transpose(pos)[0:SUBLANES, :]
    cnt_ref[0] = jnp.sum(sel, axis=0, keepdims=True)


def _route_call(x2, mod, g2, rw, T, tiles_per_batch):
    N, D = x2.shape
    E = rw.shape[1]
    assert E <= SUBLANES
    nT = N // T
    rw_pad = jnp.zeros((D, LANES), F32).at[:, :E].set(rw)
    rw_hi = rw_pad.astype(BF16)
    rw_pad = jnp.concatenate([rw_hi, (rw_pad - rw_hi.astype(F32)).astype(BF16)], axis=1)
    per_batch = mod.shape[0] > 1
    tile = pl.BlockSpec((T, D), lambda j: (j, 0))
    col = pl.BlockSpec((T, LANES), lambda j: (j, 0))
    return pl.pallas_call(
        functools.partial(_route_kernel, T=T, E=E),
        grid=(nT,),
        in_specs=[tile,
                  pl.BlockSpec((1, 6, D), (lambda j: (j // tiles_per_batch, 0, 0)) if per_batch else (lambda j: (0, 0, 0))),
                  _const_spec((1, D)), _const_spec((D, 2 * LANES))],
        out_specs=[tile, col, pl.BlockSpec((1, SUBLANES, T), lambda j: (j, 0, 0)), col,
                   pl.BlockSpec((1, 1, LANES), lambda j: (j, 0, 0))],
        out_shape=[jax.ShapeDtypeStruct((N, D), BF16), jax.ShapeDtypeStruct((N, LANES), F32),
                   jax.ShapeDtypeStruct((nT, SUBLANES, T), F32), jax.ShapeDtypeStruct((N, LANES), F32),
                   jax.ShapeDtypeStruct((nT, 1, LANES), F32)],
        compiler_params=_params(1),
        name="moe_route",
    )(x2, mod, g2, rw_pad)


def _compact_kernel(cnt_ref, start_ref, hb_ref, post_ref, hs_in_ref, hs_ref, buf, sem, busy, *, T, E, CH):
    del hs_in_ref
    j = pl.program_id(0)
    nj = pl.num_programs(0)

    @pl.when(j == 0)
    def _():
        for e in range(E):
            busy[e] = 0

    def copy(e, row):
        return pltpu.make_async_copy(buf.at[e], hs_ref.at[pl.ds(row, CH), :], sem.at[e])

    hbt = hb_ref[...]
    riota = lax.broadcasted_iota(jnp.int32, (CH, T), 0).astype(F32)
    for e in range(E):
        n = cnt_ref[j * E + e]
        s0 = start_ref[j * E + e]
        pe = post_ref[0, e:e + 1, :]

        def chunk(c, carry, e=e, pe=pe, s0=s0):
            base = c * CH

            @pl.when(busy[e] == 1)
            def _():
                copy(e, 0).wait()

            onehot = jnp.where(pe == riota + base.astype(F32), 1.0, 0.0).astype(BF16)
            buf[e] = jnp.dot(onehot, hbt, preferred_element_type=F32).astype(BF16)
            copy(e, pl.multiple_of(s0 + base, BF16_ROWS)).start()
            busy[e] = 1
            return carry

        lax.fori_loop(0, (n + CH - 1) // CH, chunk, 0)

    @pl.when(j == nj - 1)
    def _():
        for e in range(E):
            @pl.when(busy[e] == 1)
            def _():
                copy(e, 0).wait()


def _compact_call(cnt, start, hb, post, hs0, T, E, CH):
    N, D = hb.shape
    nT = N // T
    grid_spec = pltpu.PrefetchScalarGridSpec(
        num_scalar_prefetch=2,
        grid=(nT,),
        in_specs=[pl.BlockSpec((T, D), lambda j, *_: (j, 0)),
                  pl.BlockSpec((1, SUBLANES, T), lambda j, *_: (j, 0, 0)),
                  pl.BlockSpec(memory_space=pl.ANY)],
        out_specs=pl.BlockSpec(memory_space=pl.ANY),
        scratch_shapes=[pltpu.VMEM((E, CH, D), BF16), pltpu.SemaphoreType.DMA((E,)), pltpu.SMEM((E,), jnp.int32)],
    )
    return pl.pallas_call(
        functools.partial(_compact_kernel, T=T, E=E, CH=CH),
        grid_spec=grid_spec,
        out_shape=jax.ShapeDtypeStruct(hs0.shape, BF16),
        input_output_aliases={4: 0},
        compiler_params=_params(1),
        name="moe_compact",
    )(cnt, start, hb, post, hs0)


def _gffn_kernel(te_ref, rows_ref, hs_ref, w1_ref, w3_ref, w2_ref, ys_ref, *, FC, TM):
    del te_ref
    n = rows_ref[pl.program_id(0)]
    half = TM // 2
    w = (w1_ref.at[0], w3_ref.at[0], w2_ref.at[0])

    @pl.when(n > half)
    def _():
        ys_ref[...] = _swiglu_rows(hs_ref[...], *w, FC).astype(BF16)

    @pl.when((n > 0) & (n <= half))
    def _():
        ys_ref[0:half, :] = _swiglu_rows(hs_ref[0:half, :], *w, FC).astype(BF16)
        ys_ref[half:TM, :] = jnp.zeros((TM - half, ys_ref.shape[1]), BF16)

    @pl.when(n == 0)
    def _():
        ys_ref[...] = jnp.zeros(ys_ref.shape, BF16)


def _gffn_call(tile_expert, tile_rows, hs, w1, w3, w2, TM):
    R, D = hs.shape
    E, _, F = w1.shape
    grid_spec = pltpu.PrefetchScalarGridSpec(
        num_scalar_prefetch=2,
        grid=(R // TM,),
        in_specs=[pl.BlockSpec((TM, D), lambda i, te, va: (i, 0)),
                  pl.BlockSpec((1, D, F), lambda i, te, va: (te[i], 0, 0)),
                  pl.BlockSpec((1, D, F), lambda i, te, va: (te[i], 0, 0)),
                  pl.BlockSpec((1, F, D), lambda i, te, va: (te[i], 0, 0))],
        out_specs=pl.BlockSpec((TM, D), lambda i, te, va: (i, 0)),
    )
    return pl.pallas_call(
        functools.partial(_gffn_kernel, FC=_ff_chunk(F), TM=TM),
        grid_spec=grid_spec,
        out_shape=jax.ShapeDtypeStruct((R, D), BF16),
        compiler_params=_params(1),
        name="moe_grouped_swiglu",
    )(tile_expert, tile_rows, hs, w1, w3, w2)


def _combine_kernel(cnt_ref, start_ref, x_ref, mod_ref, pos_ref, comb_ref, ys_ref, *rest, T, E, CH, final_norm):
    if final_norm:
        fg_ref, o_ref, ybuf, xbuf, acc, sem, xsem = rest
    else:
        o_ref, ybuf, xbuf, acc, sem, xsem = rest
    j = pl.program_id(0)
    nj = pl.num_programs(0)
    slot = j % 2

    def first_chunk(jj, sl, e):
        row = pl.multiple_of(start_ref[jj * E + e], BF16_ROWS)
        return pltpu.make_async_copy(ys_ref.at[pl.ds(row, CH), :], ybuf.at[sl, e], sem.at[sl, e])

    @pl.when(j == 0)
    def _():
        for e in range(E):
            first_chunk(0, 0, e).start()

    @pl.when(j + 1 < nj)
    def _():
        for e in range(E):
            first_chunk(j + 1, 1 - slot, e).start()

    liota = lax.broadcasted_iota(jnp.int32, (T, CH), 1).astype(F32)
    for e in range(E):
        n = cnt_ref[j * E + e]
        s0 = start_ref[j * E + e]
        pe = pos_ref[:, e:e + 1]
        we = comb_ref[:, e:e + 1]
        first_chunk(j, slot, e).wait()
        onehot = jnp.where(pe == liota, 1.0, 0.0).astype(BF16)
        part = we * jnp.dot(onehot, ybuf[slot, e], preferred_element_type=F32)
        if e == 0:
            acc[...] = part
        else:
            acc[...] += part

        def extra(c, carry, pe=pe, we=we, s0=s0):
            base = c * CH
            cp = pltpu.make_async_copy(ys_ref.at[pl.ds(pl.multiple_of(s0 + base, BF16_ROWS), CH), :], xbuf, xsem)
            cp.start()
            cp.wait()
            onehot_c = jnp.where(pe == liota + base.astype(F32), 1.0, 0.0).astype(BF16)
            acc[...] += we * jnp.dot(onehot_c, xbuf[...], preferred_element_type=F32)
            return carry

        lax.fori_loop(1, (n + CH - 1) // CH, extra, 0)
    y = x_ref[...] + mod_ref[0, 5:6, :] * acc[...]
    if final_norm:
        y = y * lax.rsqrt(jnp.mean(y * y, axis=-1, keepdims=True) + EPS) * fg_ref[...]
    o_ref[...] = y


def _combine_call(cnt, start, x2, mod, pos, comb, ys, T, E, CH, tiles_per_batch, final_g=None):
    N, D = x2.shape
    per_batch = mod.shape[0] > 1
    tile = pl.BlockSpec((T, D), lambda j, *_: (j, 0))
    col = pl.BlockSpec((T, LANES), lambda j, *_: (j, 0))
    in_specs = [tile,
                pl.BlockSpec((1, 6, D), (lambda j, *_: (j // tiles_per_batch, 0, 0)) if per_batch
                             else (lambda j, *_: (0, 0, 0))),
                col, col, pl.BlockSpec(memory_space=pl.ANY)]
    args = [cnt, start, x2, mod, pos, comb, ys]
    if final_g is not None:
        in_specs.append(pl.BlockSpec((1, D), lambda j, *_: (0, 0)))
        args.append(final_g)
    grid_spec = pltpu.PrefetchScalarGridSpec(
        num_scalar_prefetch=2,
        grid=(N // T,),
        in_specs=in_specs,
        out_specs=tile,
        scratch_shapes=[pltpu.VMEM((2, E, CH, D), BF16), pltpu.VMEM((CH, D), BF16), pltpu.VMEM((T, D), F32),
                        pltpu.SemaphoreType.DMA((2, E)), pltpu.SemaphoreType.DMA(())],
    )
    return pl.pallas_call(
        functools.partial(_combine_kernel, T=T, E=E, CH=CH, final_norm=final_g is not None),
        grid_spec=grid_spec,
        out_shape=jax.ShapeDtypeStruct((N, D), F32),
        compiler_params=_params(1),
        name="moe_combine",
    )(*args)


def _moe(x, mod, g2, rw, w1, w3, w2, final_g=None):
    B, L, D = x.shape
    N = B * L
    E = rw.shape[1]
    T = min(512, L)
    CH = min(256, T)
    TM = 512
    nT = N // T
    x2 = x.reshape(N, D)
    hb, pos, post, comb, cnt_f = _route_call(x2, mod, g2, rw, T, L // T)

    cnt = cnt_f[:, 0, :E].astype(jnp.int32)
    npad = (cnt + BF16_ROWS - 1) // BF16_ROWS * BF16_ROWS
    glen = (npad.sum(0) + CH + TM - 1) // TM * TM
    gend = jnp.cumsum(glen)
    start = (gend - glen)[None, :] + jnp.cumsum(npad, axis=0) - npad
    rows_bound = 2 * N + nT * E * BF16_ROWS + E * (CH + TM)
    R = (rows_bound + TM - 1) // TM * TM
    tile_row = jnp.arange(R // TM, dtype=jnp.int32) * TM
    tile_expert = jnp.minimum(jnp.sum(tile_row[:, None] >= gend[None, :], axis=1), E - 1).astype(jnp.int32)
    data_end = (gend - glen + npad.sum(0))[tile_expert]
    tile_rows = jnp.clip(data_end - tile_row, 0, TM).astype(jnp.int32)
    cnt1, start1 = cnt.reshape(-1), start.reshape(-1).astype(jnp.int32)

    hs = _compact_call(cnt1, start1, hb, post, jnp.zeros((R, D), BF16), T, E, CH)
    ys = _gffn_call(tile_expert, tile_rows, hs, w1, w3, w2, TM)
    out = _combine_call(cnt1, start1, x2, mod, pos, comb, ys, T, E, CH, L // T, final_g)
    return out.reshape(B, L, D)


def _pack_gates(wr, wi):
    H, hd, _ = wr.shape
    per = GATE_GROUP // hd
    G = H // per

    def blockdiag(w):
        w = w.reshape(G, per, hd, hd)
        eye = jnp.eye(per, dtype=w.dtype)
        return jnp.einsum('gpij,pq->gpiqj', w, eye).reshape(G, GATE_GROUP, GATE_GROUP)

    return (0.5 * jnp.concatenate([blockdiag(wr), blockdiag(wi)], axis=-1)).astype(BF16)


def _mixer(x, mod, p, stream_tiles, conv_stride, h0_f, h0_b, full):
    T_in, T_rnn, T_cv, HB = stream_tiles
    uc, gg, v, ga, gb = _inproj_call(x, mod, p['g1'], p['w_in'], p['rnn_cw'], p['rnn_cb'], T_in)
    rnn_f = (uc, p['wg'][0], p['br'][0], p['bi'][0], p['lam'][0], h0_f, T_rnn)
    rnn_b = (uc, p['wg'][1], p['br'][1], p['bi'][1], p['lam'][1], h0_b, T_rnn)
    if not full:
        (s_f,) = _rnn_call(*rnn_f, reverse=False, write_h=False)
        (s_b,) = _rnn_call(*rnn_b, reverse=True, write_h=False)
        return None, s_f, s_b
    ybg = _convb_call(v, gb, p['cw'], p['cb'], p['lg'], p['lb'], p['w_b'], T_cv, HB, conv_stride)
    hf, s_f = _rnn_call(*rnn_f, reverse=False)
    s_b, x_new = _rnn_call(*rnn_b, reverse=True, write_h=False,
                           final_args=(hf, gg, ga, ybg, x, mod, p['w_a'], p['w_out']))
    return x_new, s_f, s_b


def kernel(x, c, ctx, c_ctx, mod_w, mod_b, norm1_g, norm2_g, w_in, rnn_conv_w, rnn_conv_b, lru_wr, lru_br, lru_wi, lru_bi, lru_lam, conv_w, conv_b, conv_ln_g, conv_ln_b, w_branch_a, w_branch_b, w_out, ffn_w1, ffn_w3, ffn_w2, moe_router, moe_w1, moe_w3, moe_w2, final_g):
    B, L, D = x.shape
    Lc = ctx.shape[1]
    depth = mod_w.shape[0]
    assert B + 1 <= SUBLANES and D % GATE_GROUP == 0 and GATE_GROUP % lru_wr.shape[-1] == 0

    rows = jnp.zeros((SUBLANES, D), F32).at[:B].set(c).at[B].set(c_ctx)
    mod_all = _mod_call(rows, mod_w, mod_b)

    lat_tiles = (min(512, L), min(512, L), min(1024, L), min(1024, L))
    ctx_tiles = (Lc, Lc, Lc, BF16_ROWS)
    T_ffn = min(512, L)
    zero_state = jnp.zeros((1, 1, D), F32)
    fg = final_g.reshape(1, D)
    in_col_scale = jnp.repeat(jnp.array([1.0, 1.0, 0.5, 0.5, 0.5, 0.5], F32), D)[None, :]

    for i in range(depth):
        last = i == depth - 1
        mod_x = mod_all[i, :B].reshape(B, 6, D)
        mod_c = mod_all[i, B].reshape(1, 6, D)
        p = dict(
            g1=norm1_g[i].reshape(1, D), w_in=(w_in[i] * in_col_scale).astype(BF16),
            rnn_cw=rnn_conv_w[i], rnn_cb=rnn_conv_b[i].reshape(1, D),
            wg=[_pack_gates(lru_wr[i, d], lru_wi[i, d]) for d in range(2)],
            br=[0.5 * lru_br[i, d].reshape(1, D) for d in range(2)],
            bi=[0.5 * lru_bi[i, d].reshape(1, D) for d in range(2)],
            lam=[lru_lam[i, d].reshape(1, D) for d in range(2)],
            cw=conv_w[i], cb=conv_b[i].reshape(1, D), lg=conv_ln_g[i].reshape(1, D), lb=conv_ln_b[i].reshape(1, D),
            w_a=w_branch_a[i].astype(BF16), w_b=w_branch_b[i].astype(BF16), w_out=w_out[i].astype(BF16))
        g2 = norm2_g[i].reshape(1, D)
        j = i // 2
        dense = i % 2 == 0
        if dense:
            ffn_w = (ffn_w1[j].astype(BF16), ffn_w3[j].astype(BF16), ffn_w2[j].astype(BF16))
        else:
            ffn_w = (moe_router[j], moe_w1[j].astype(BF16), moe_w3[j].astype(BF16), moe_w2[j].astype(BF16))

        ctx_new, s_f, s_b = _mixer(ctx, mod_c, p, ctx_tiles, 1, zero_state, zero_state, full=not last)
        if not last:
            ctx = (_ffn_call(ctx_new, mod_c, g2, *ffn_w, Lc) if dense
                   else _moe(ctx_new, mod_c, g2, *ffn_w))
        x, _, _ = _mixer(x, mod_x, p, lat_tiles, GRID_W, s_f, s_b, full=True)
        fgi = fg if last else None
        x = (_ffn_call(x, mod_x, g2, *ffn_w, T_ffn, fgi) if dense else _moe(x, mod_x, g2, *ffn_w, fgi))
    return x
```
